```python
import jax, jax.numpy as jnp
from jax import lax
import numpy as np

D_MODEL = 1024
BATCH = 2
SEQ = 8192
DEPTH = 4
DEC_BATCH = 128
DEC_SEQ = 4
PAST_LEN = 2048
PAGE_SIZE = 128

D_FF = 2816
D_A = 512
CONV_A = 31
D_B = 512
CONV_B = 3
N_HEADS = 16
N_KV = 4
GROUP = N_HEADS // N_KV
HEAD_DIM = 64
L_CMP = 32
D_CMP = 16
L_SEL = 64
N_SEL = 16
N_LOCAL = 2
WINDOW = 512
Q_BLOCK = 128
N_BRANCH = 3
EPS = 1e-6
NEG = -1e30
FORCE = 1e4

kernel_name = "macaron_gated_conv_nsa_decoder"


def rmsnorm(x, g):
    xf = x.astype(jnp.float32)
    r = lax.rsqrt(jnp.mean(xf * xf, axis=-1, keepdims=True) + EPS)
    return (xf * r).astype(x.dtype) * g


def masked_softmax(s, mask):
    s = jnp.where(mask, s, NEG)
    p = jnp.where(mask, jnp.exp(s - jnp.max(s, axis=-1, keepdims=True)), 0.0)
    den = jnp.sum(p, axis=-1, keepdims=True)
    return p / jnp.where(den > 0, den, 1.0)


def alibi_slopes():
    s = np.exp2(-8.0 * np.arange(1, N_HEADS + 1) / N_HEADS).astype(np.float32)
    return jnp.asarray(s.reshape(N_KV, GROUP))


def ffn_half(x, g, w_in, w_out):
    h = rmsnorm(x, g)
    u, v = jnp.split(h @ w_in, 2, axis=-1)
    return x + 0.5 * ((jax.nn.silu(u) * v) @ w_out)


def split_proj(h, w_in):
    z = h @ w_in
    sizes = (2 * D_A, 3 * D_B, N_HEADS * HEAD_DIM, 6 * N_KV * HEAD_DIM, 3 * N_HEADS, N_BRANCH * D_MODEL)
    offs = [int(v) for v in np.cumsum(sizes)[:-1]]
    return jnp.split(z, offs, axis=-1)


def causal_dwconv(u, buf, w):
    up = jnp.concatenate([buf, u], axis=1)
    y = lax.conv_general_dilated(up, w[:, None, :], window_strides=(1,), padding='VALID',
                                 dimension_numbers=('NWC', 'WIO', 'NWC'),
                                 feature_group_count=u.shape[-1])
    return y, up[:, -buf.shape[1]:]


def conformer_conv(za, buf, lp):
    a = za[..., :D_A] * jax.nn.sigmoid(za[..., D_A:])
    c, new_buf = causal_dwconv(a, buf, lp["w_conv_a"])
    c = jax.nn.silu(rmsnorm(c + lp["b_conv_a"], lp["norm_conv_a"]))
    return c @ lp["w_out_a"], new_buf


def short_conv(zb, buf, lp):
    bg, cg, xv = jnp.split(zb, 3, axis=-1)
    c, new_buf = causal_dwconv(cg * xv, buf, lp["w_conv_b"])
    return (bg * c) @ lp["w_out_b"], new_buf


def nsa_project(zq, zkv, zg, lp):
    N, T = zq.shape[0], zq.shape[1]
    q = rmsnorm(zq.reshape(N, T, N_KV, GROUP, HEAD_DIM), lp["norm_q"]) * HEAD_DIM ** -0.5
    kv = zkv.reshape(N, T, 3, 2, N_KV, HEAD_DIM)
    kv_cmp = kv[:, :, 0]
    kv_sel = jnp.stack([rmsnorm(kv[:, :, 1, 0], lp["norm_k"][1]), kv[:, :, 1, 1]], axis=2)
    kv_win = jnp.stack([rmsnorm(kv[:, :, 2, 0], lp["norm_k"][2]), kv[:, :, 2, 1]], axis=2)
    gates = jax.nn.sigmoid(zg).reshape(N, T, N_KV, GROUP, 3)
    return q, kv_cmp, kv_sel, kv_win, gates


def compress(kv, lp):
    N, T = kv.shape[0], kv.shape[1]
    r = L_CMP // D_CMP
    n_chunks = T // D_CMP
    nc = n_chunks - r + 1
    ch = kv[:, :n_chunks * D_CMP].reshape(N, n_chunks, D_CMP, 2, N_KV, HEAD_DIM)
    blocks = jnp.concatenate([ch[:, i:i + nc] for i in range(r)], axis=2) + lp["pe_cmp"][:, :, None, :]
    comp = jnp.einsum('nclegd,eldf->ncegf', blocks, lp["w_phi"])
    kc = rmsnorm(comp[:, :, 0], lp["norm_k"][0])
    vc = comp[:, :, 1]
    c_end = jnp.arange(nc, dtype=jnp.int32) * D_CMP + (L_CMP - 1)
    return kc, vc, c_end


def to_sel_blocks(kv):
    N, T = kv.shape[0], kv.shape[1]
    n_sb = -(-T // L_SEL)
    kv = jnp.pad(kv, ((0, 0), (0, n_sb * L_SEL - T), (0, 0), (0, 0), (0, 0)))
    b = kv.reshape(N, n_sb, L_SEL, 2, N_KV, HEAD_DIM).transpose(3, 0, 4, 1, 2, 5)
    return b[0], b[1]


def cmp_sel_overlap(nc, n_sb):
    c = np.arange(nc)[:, None] * D_CMP
    j = np.arange(n_sb)[None, :] * L_SEL
    ov = (c < j + L_SEL) & (c + L_CMP > j)
    return jnp.asarray(ov.astype(np.float32))


def gather_pages(pool, page_table):
    g = pool[page_table]
    return g.reshape((g.shape[0], g.shape[1] * g.shape[2]) + g.shape[3:])


def nsa_core(q, t_pos, kc, vc, c_end, kb, vb, kw, vw, s_pos, gates, slopes, ov):
    N, Tq = q.shape[0], q.shape[1]
    f32 = jnp.float32
    sl = slopes[None, None, :, :, None]
    dist_c = t_pos[:, None] - c_end[None, :]
    sc = jnp.einsum('ntgrd,ncgd->ntgrc', q, kc).astype(f32) - sl * dist_c[None, :, None, None, :].astype(f32)
    pc = masked_softmax(sc, (dist_c >= 0)[None, :, None, None, :])
    o_cmp = jnp.einsum('ntgrc,ncgd->ntgrd', pc.astype(vc.dtype), vc)
    n_sb = ov.shape[1]
    imp = jnp.einsum('ntgrc,cj->ntgj', pc, ov)
    j = jnp.arange(n_sb, dtype=jnp.int32)[None, :]
    cur = (t_pos // L_SEL)[:, None]
    valid_b = (j <= cur)[None, :, None, :]
    forced = ((j == 0) | ((j <= cur) & (j > cur - N_LOCAL)))[None, :, None, :]
    imp = jnp.where(forced, FORCE, jnp.where(valid_b, imp, -1.0))
    top_s, idx = lax.top_k(imp, min(N_SEL, n_sb))
    k_sel = idx.shape[-1]
    nidx = jnp.arange(N)[:, None, None, None]
    gidx = jnp.arange(N_KV)[None, None, :, None]
    ks = kb[nidx, gidx, idx].reshape(N, Tq, N_KV, k_sel * L_SEL, HEAD_DIM)
    vs = vb[nidx, gidx, idx].reshape(N, Tq, N_KV, k_sel * L_SEL, HEAD_DIM)
    s_abs = idx[..., None] * L_SEL + jnp.arange(L_SEL, dtype=jnp.int32)
    dist_s = t_pos[None, :, None, None, None] - s_abs
    mask_s = ((dist_s >= 0) & (top_s >= 0)[..., None]).reshape(N, Tq, N_KV, 1, k_sel * L_SEL)
    dist_s = dist_s.reshape(N, Tq, N_KV, 1, k_sel * L_SEL).astype(f32)
    ss = jnp.einsum('ntgrd,ntgsd->ntgrs', q, ks).astype(f32) - slopes[None, None, :, :, None] * dist_s
    ps = masked_softmax(ss, mask_s)
    o_sel = jnp.einsum('ntgrs,ntgsd->ntgrd', ps.astype(vs.dtype), vs)
    dist_w = t_pos[:, None] - s_pos[None, :]
    mask_w = (dist_w >= 0) & (dist_w <= WINDOW) & (s_pos >= 0)[None, :]
    sw = jnp.einsum('ntgrd,nsgd->ntgrs', q, kw).astype(f32) - sl * dist_w[None, :, None, None, :].astype(f32)
    pw = masked_softmax(sw, mask_w[None, :, None, None, :])
    o_win = jnp.einsum('ntgrs,nsgd->ntgrd', pw.astype(vw.dtype), vw)
    return gates[..., 0:1] * o_cmp + gates[..., 1:2] * o_sel + gates[..., 2:3] * o_win


def merge(zm, ya, yb, yc, w_o):
    g = jax.nn.sigmoid(zm).reshape(zm.shape[:-1] + (N_BRANCH, D_MODEL))
    return (g[..., 0, :] * ya + g[..., 1, :] * yb + g[..., 2, :] * yc) @ w_o


def mix_prompt(h, lp, slopes):
    B, T = h.shape[0], h.shape[1]
    za, zb, zq, zkv, zg, zm = split_proj(h, lp["w_in"])
    ya, buf_a = conformer_conv(za, jnp.zeros((B, CONV_A - 1, D_A), h.dtype), lp)
    yb, buf_b = short_conv(zb, jnp.zeros((B, CONV_B - 1, D_B), h.dtype), lp)
    q, kv_cmp, kv_sel, kv_win, gates = nsa_project(zq, zkv, zg, lp)
    kc, vc, c_end = compress(kv_cmp, lp)
    kb, vb = to_sel_blocks(kv_sel)
    ov = cmp_sel_overlap(kc.shape[1], kb.shape[2])
    n_qb = T // Q_BLOCK
    r_w = WINDOW // Q_BLOCK + 1
    pad = jnp.zeros((B, WINDOW) + kv_win.shape[2:], kv_win.dtype)
    pb = jnp.concatenate([pad, kv_win], axis=1).reshape(B, n_qb + r_w - 1, Q_BLOCK, 2, N_KV, HEAD_DIM)
    kw_blocks = jnp.moveaxis(jnp.concatenate([pb[:, i:i + n_qb] for i in range(r_w)], axis=2), 1, 0)
    s_pos = (jnp.arange(n_qb, dtype=jnp.int32)[:, None] * Q_BLOCK - WINDOW
             + jnp.arange(r_w * Q_BLOCK, dtype=jnp.int32)[None, :])
    t_pos = jnp.arange(T, dtype=jnp.int32).reshape(n_qb, Q_BLOCK)

    def to_blocks(a):
        return jnp.moveaxis(a.reshape((B, n_qb, Q_BLOCK) + a.shape[2:]), 1, 0)

    def block_fn(args):
        qb, tb, kwb, sb, gb = args
        return nsa_core(qb, tb, kc, vc, c_end, kb, vb, kwb[:, :, 0], kwb[:, :, 1], sb, gb, slopes, ov)

    o = lax.map(block_fn, (to_blocks(q), t_pos, kw_blocks, s_pos, to_blocks(gates)))
    o = jnp.moveaxis(o, 0, 1).reshape(B, T, N_HEADS * HEAD_DIM)
    y = merge(zm, ya, yb, o @ lp["w_out_c"], lp["w_o"])
    return y, (kv_cmp, kv_sel, kv_win[:, -min(WINDOW, T):], buf_a, buf_b)


def mix_sample(h, lp, slopes, pool_cmp, pool_sel, win_buf, buf_a, buf_b, page_table):
    N, T = h.shape[0], h.shape[1]
    past = page_table.shape[1] * PAGE_SIZE
    w_buf = win_buf.shape[1]
    za, zb, zq, zkv, zg, zm = split_proj(h, lp["w_in"])
    ya, new_a = conformer_conv(za, buf_a, lp)
    yb, new_b = short_conv(zb, buf_b, lp)
    q, kv_cmp, kv_sel, kv_win, gates = nsa_project(zq, zkv, zg, lp)
    all_cmp = jnp.concatenate([gather_pages(pool_cmp, page_table), kv_cmp], axis=1)
    kc, vc, c_end = compress(all_cmp, lp)
    all_sel = jnp.concatenate([gather_pages(pool_sel, page_table), kv_sel], axis=1)
    kb, vb = to_sel_blocks(all_sel)
    ov = cmp_sel_overlap(kc.shape[1], kb.shape[2])
    kw = jnp.concatenate([win_buf, kv_win], axis=1)
    s_pos = past - w_buf + jnp.arange(w_buf + T, dtype=jnp.int32)
    t_pos = past + jnp.arange(T, dtype=jnp.int32)
    o = nsa_core(q, t_pos, kc, vc, c_end, kb, vb, kw[:, :, 0], kw[:, :, 1], s_pos, gates, slopes, ov)
    y = merge(zm, ya, yb, o.reshape(N, T, N_HEADS * HEAD_DIM) @ lp["w_out_c"], lp["w_o"])
    return y, (kv_cmp, kv_sel, kw[:, -w_buf:], new_a, new_b)


def setup_inputs(seed: int = 0) -> dict:
    key = jax.random.key(seed)
    ks = jax.random.split(key, 24)
    f32 = jnp.float32

    def nrm(k, shape, scale=1.0):
        return jax.random.normal(k, shape, f32) * scale

    def gain(k, shape):
        return 1.0 + 0.02 * jax.random.normal(k, shape, f32)

    n_pages = PAST_LEN // PAGE_SIZE
    n_used = DEC_BATCH * n_pages
    n_pool = n_used + -(-n_used // 4)
    page_table = jax.random.permutation(ks[0], n_pool)[:n_used].reshape(DEC_BATCH, n_pages).astype(jnp.int32)
    w_buf = min(WINDOW, PAST_LEN)
    d_in = 2 * D_A + 3 * D_B + N_HEADS * HEAD_DIM + 6 * N_KV * HEAD_DIM + 3 * N_HEADS + N_BRANCH * D_MODEL
    return {
        "x_prompt": nrm(ks[1], (BATCH, SEQ, D_MODEL)),
        "x_sample": nrm(ks[2], (DEC_BATCH, DEC_SEQ, D_MODEL)),
        "cache_cmp_kv": nrm(ks[3], (DEPTH, n_pool, PAGE_SIZE, 2, N_KV, HEAD_DIM)),
        "cache_sel_kv": nrm(ks[4], (DEPTH, n_pool, PAGE_SIZE, 2, N_KV, HEAD_DIM)),
        "state_win_kv": nrm(ks[5], (DEPTH, DEC_BATCH, w_buf, 2, N_KV, HEAD_DIM)),
        "state_conv_a": nrm(ks[6], (DEPTH, DEC_BATCH, CONV_A - 1, D_A), 0.5),
        "state_conv_b": nrm(ks[7], (DEPTH, DEC_BATCH, CONV_B - 1, D_B), 0.5),
        "page_table": page_table,
        "norm_ffn": gain(ks[8], (DEPTH, 2, D_MODEL)),
        "w_ffn_in": nrm(ks[9], (DEPTH, 2, D_MODEL, 2 * D_FF), D_MODEL ** -0.5),
        "w_ffn_out": nrm(ks[10], (DEPTH, 2, D_FF, D_MODEL), D_FF ** -0.5),
        "norm_mix": gain(ks[11], (DEPTH, D_MODEL)),
        "w_in": nrm(ks[12], (DEPTH, D_MODEL, d_in), D_MODEL ** -0.5),
        "w_conv_a": nrm(ks[13], (DEPTH, CONV_A, D_A), CONV_A ** -0.5),
        "b_conv_a": nrm(ks[14], (DEPTH, D_A), 0.02),
        "norm_conv_a": gain(ks[15], (DEPTH, D_A)),
        "w_out_a": nrm(ks[16], (DEPTH, D_A, D_MODEL), D_A ** -0.5),
        "w_conv_b": nrm(ks[17], (DEPTH, CONV_B, D_B), CONV_B ** -0.5),
        "w_out_b": nrm(ks[18], (DEPTH, D_B, D_MODEL), D_B ** -0.5),
        "norm_q": gain(ks[19], (DEPTH, HEAD_DIM)),
        "norm_k": gain(ks[20], (DEPTH, 3, HEAD_DIM)),
        "pe_cmp": nrm(ks[21], (DEPTH, L_CMP, 2, HEAD_DIM), 0.1),
        "w_phi": nrm(ks[22], (DEPTH, 2, L_CMP, HEAD_DIM, HEAD_DIM), (L_CMP * HEAD_DIM) ** -0.5),
        "w_out_c": nrm(ks[23], (DEPTH, N_HEADS * HEAD_DIM, D_MODEL), (N_HEADS * HEAD_DIM) ** -0.5),
        "w_o": nrm(jax.random.fold_in(key, 99), (DEPTH, D_MODEL, D_MODEL), D_MODEL ** -0.5),
    }


def reference(x_prompt, x_sample, cache_cmp_kv, cache_sel_kv, state_win_kv, state_conv_a, state_conv_b,
              page_table, norm_ffn, w_ffn_in, w_ffn_out, norm_mix, w_in, w_conv_a, b_conv_a, norm_conv_a,
              w_out_a, w_conv_b, w_out_b, norm_q, norm_k, pe_cmp, w_phi, w_out_c, w_o):
    slopes = alibi_slopes()
    xp, xs = x_prompt, x_sample
    sp_lists = [[] for _ in range(5)]
    ss_lists = [[] for _ in range(5)]
    for l in range(DEPTH):
        lp = {"w_in": w_in[l], "w_conv_a": w_conv_a[l], "b_conv_a": b_conv_a[l],
              "norm_conv_a": norm_conv_a[l], "w_out_a": w_out_a[l], "w_conv_b": w_conv_b[l],
              "w_out_b": w_out_b[l], "norm_q": norm_q[l], "norm_k": norm_k[l], "pe_cmp": pe_cmp[l],
              "w_phi": w_phi[l], "w_out_c": w_out_c[l], "w_o": w_o[l]}
        xp = ffn_half(xp, norm_ffn[l, 0], w_ffn_in[l, 0], w_ffn_out[l, 0])
        xs = ffn_half(xs, norm_ffn[l, 0], w_ffn_in[l, 0], w_ffn_out[l, 0])
        yp, sp = mix_prompt(rmsnorm(xp, norm_mix[l]), lp, slopes)
        ys, ss = mix_sample(rmsnorm(xs, norm_mix[l]), lp, slopes, cache_cmp_kv[l], cache_sel_kv[l],
                            state_win_kv[l], state_conv_a[l], state_conv_b[l], page_table)
        xp = xp + yp
        xs = xs + ys
        xp = ffn_half(xp, norm_ffn[l, 1], w_ffn_in[l, 1], w_ffn_out[l, 1])
        xs = ffn_half(xs, norm_ffn[l, 1], w_ffn_in[l, 1], w_ffn_out[l, 1])
        for i in range(5):
            sp_lists[i].append(sp[i])
            ss_lists[i].append(ss[i])
    return (xp, xs,
            jnp.stack(sp_lists[0]), jnp.stack(ss_lists[0]),
            jnp.stack(sp_lists[1]), jnp.stack(ss_lists[1]),
            jnp.stack(sp_lists[2]), jnp.stack(ss_lists[2]),
            jnp.stack(sp_lists[3]), jnp.stack(ss_lists[3]),
            jnp.stack(sp_lists[4]), jnp.stack(ss_lists[4]))
```

```python
import functools

import numpy as np
import jax
import jax.numpy as jnp
from jax import lax
from jax.experimental import pallas as pl
from jax.experimental.pallas import tpu as pltpu

F32 = jnp.float32
BF16 = jnp.bfloat16

D_A = 512
CONV_A = 31
D_B = 512
CONV_B = 3
N_HEADS = 16
N_KV = 4
GROUP = N_HEADS // N_KV
HEAD_DIM = 64
KV_W = N_KV * HEAD_DIM
L_CMP = 32
D_CMP = 16
L_SEL = 64
N_SEL = 16
N_LOCAL = 2
WINDOW = 512
Q_BLOCK = 128
N_BRANCH = 3
PAGE_SIZE = 128
EPS = 1e-6
NEG = -1e30
FORCE = 1e4
SLOPES = [float(v) for v in np.exp2(-8.0 * np.arange(1, N_HEADS + 1) / N_HEADS).astype(np.float32)]

LANES = 128
SEL_TILE = 512
N_FF_TILES = 2
SAMPLES_PER_COMPRESS_STEP = 4
VMEM_LIMIT = 52 * 1024 * 1024


def _dot(a, b):
    return jnp.dot(a, b, preferred_element_type=F32)


def _dot_nt(a, b):
    return lax.dot_general(a, b, (((1,), (1,)), ((), ())), preferred_element_type=F32)


def _dot_split(a, b):
    hi = a.astype(BF16)
    lo = (a - hi.astype(F32)).astype(BF16)
    return _dot(hi, b) + _dot(lo, b)


def _sigmoid(x):
    return 1.0 / (1.0 + jnp.exp(-x))


def _rms(x, g):
    r = lax.rsqrt(jnp.mean(x * x, axis=-1, keepdims=True) + EPS)
    return (x * r) * g


def _head_rms(z, g):
    rows, n = z.shape
    lo = lax.broadcasted_iota(jnp.int32, (rows, LANES), 1) < HEAD_DIM
    outs = []
    for c in range(n // LANES):
        zc = z[:, c * LANES:(c + 1) * LANES]
        sq = zc * zc
        s_lo = jnp.sum(jnp.where(lo, sq, 0.0), axis=1, keepdims=True)
        s_hi = jnp.sum(jnp.where(lo, 0.0, sq), axis=1, keepdims=True)
        ms = jnp.where(lo, s_lo, s_hi) * (1.0 / HEAD_DIM)
        outs.append((zc * lax.rsqrt(ms + EPS)) * g[:, c * LANES:(c + 1) * LANES])
    return outs[0] if len(outs) == 1 else jnp.concatenate(outs, axis=1)


def _topk_mask(v, k, axis):
    n = v.shape[axis]
    idx = lax.broadcasted_iota(jnp.int32, v.shape, axis).astype(F32)
    sel = jnp.zeros(v.shape, F32)
    for _ in range(k):
        m = jnp.max(v, axis=axis, keepdims=True)
        cand = jnp.where(v == m, idx, float(n))
        first = jnp.min(cand, axis=axis, keepdims=True)
        pick = idx == first
        sel = jnp.where(pick, 1.0, sel)
        v = jnp.where(pick, -3e38, v)
    return sel


def _forced_importance(imp, blk, cur, n_sb):
    local = jnp.where(blk > cur - N_LOCAL, FORCE, imp)
    imp = jnp.where(blk == 0, FORCE, jnp.where(blk <= cur, local, -1.0))
    return jnp.where(blk < n_sb, imp, -2.0)


def _cparams(sem):
    return pltpu.CompilerParams(dimension_semantics=sem, vmem_limit_bytes=VMEM_LIMIT)


def _ffn_kernel(x_ref, g_ref, wu_ref, wv_ref, wo_ref, o_ref, h_ref, acc_ref):
    f = pl.program_id(1)

    @pl.when(f == 0)
    def _():
        h_ref[...] = _rms(x_ref[...], g_ref[...]).astype(BF16)
        acc_ref[...] = jnp.zeros_like(acc_ref)

    h = h_ref[...]
    u = _dot(h, wu_ref[...])
    v = _dot(h, wv_ref[...])
    act = (u * _sigmoid(u)) * v
    acc_ref[...] += _dot(act.astype(BF16), wo_ref[...])

    @pl.when(f == pl.num_programs(1) - 1)
    def _():
        o_ref[...] = x_ref[...] + 0.5 * acc_ref[...]


def _ffn(x, norm_ffn, w_in, w_out, l, hf):
    m, d = x.shape
    ff = w_out.shape[2]
    tf = ff // N_FF_TILES
    tm = min(512, m)
    return pl.pallas_call(
        _ffn_kernel,
        grid=(m // tm, N_FF_TILES),
        in_specs=[
            pl.BlockSpec((tm, d), lambda i, f: (i, 0)),
            pl.BlockSpec((None, None, 1, d), lambda i, f: (l, hf, 0, 0)),
            pl.BlockSpec((None, None, d, tf), lambda i, f: (l, hf, 0, f)),
            pl.BlockSpec((None, None, d, tf), lambda i, f: (l, hf, 0, N_FF_TILES + f)),
            pl.BlockSpec((None, None, tf, d), lambda i, f: (l, hf, f, 0)),
        ],
        out_specs=pl.BlockSpec((tm, d), lambda i, f: (i, 0)),
        out_shape=jax.ShapeDtypeStruct((m, d), F32),
        scratch_shapes=[pltpu.VMEM((tm, d), BF16), pltpu.VMEM((tm, d), F32)],
        compiler_params=_cparams(("parallel", "arbitrary")),
        name="ffn_half",
    )(x, norm_ffn, w_in, w_in, w_out)


def _inproj_kernel(x_ref, gm_ref, wa_ref, wb_ref, wq_ref, wkv_ref, wg_ref, nq_ref, nk1_ref, nk2_ref,
                   a_ref, bg_ref, cx_ref, q_ref, kvc_ref, kvs_ref, kvw_ref, gt_ref,
                   ks_ref, vs_ref, kw_ref, vw_ref):
    h = _rms(x_ref[...], gm_ref[...]).astype(BF16)
    za = _dot(h, wa_ref[...])
    a_ref[...] = za[:, :D_A] * _sigmoid(za[:, D_A:])
    zb = _dot(h, wb_ref[...])
    bg_ref[...] = zb[:, :D_B]
    cx_ref[...] = zb[:, D_B:2 * D_B] * zb[:, 2 * D_B:]
    q_ref[...] = _head_rms(_dot(h, wq_ref[...]), nq_ref[...]).astype(BF16)
    zkv = _dot(h, wkv_ref[...])
    kvc_ref[...] = zkv[:, :2 * KV_W]
    ks = _head_rms(zkv[:, 2 * KV_W:3 * KV_W], nk1_ref[...])
    vs = zkv[:, 3 * KV_W:4 * KV_W]
    kvs_ref[:, :KV_W] = ks
    kvs_ref[:, KV_W:] = vs
    ks_ref[...] = ks.astype(BF16)
    vs_ref[...] = vs.astype(BF16)
    kw = _head_rms(zkv[:, 4 * KV_W:5 * KV_W], nk2_ref[...])
    vw = zkv[:, 5 * KV_W:]
    kvw_ref[:, :KV_W] = kw
    kvw_ref[:, KV_W:] = vw
    kw_ref[...] = kw.astype(BF16)
    vw_ref[...] = vw.astype(BF16)
    gt_ref[...] = _sigmoid(_dot(h, wg_ref[...]))


def _inproj(x, p, l):
    m, d = x.shape
    tm = min(256, m)

    def wspec(w):
        return pl.BlockSpec((None,) + w.shape[1:], lambda i: (l, 0, 0))

    def rows(n, dt):
        return pl.BlockSpec((tm, n), lambda i: (i, 0)), jax.ShapeDtypeStruct((m, n), dt)

    outs = [rows(D_A, F32), rows(D_B, F32), rows(D_B, F32), rows(N_HEADS * HEAD_DIM, BF16),
            rows(2 * KV_W, F32), rows(2 * KV_W, F32), rows(2 * KV_W, F32), rows(LANES, F32),
            rows(KV_W, BF16), rows(KV_W, BF16), rows(KV_W, BF16), rows(KV_W, BF16)]
    ws = [p["norm_mix"], p["w_a"], p["w_b"], p["w_q"], p["w_kv"], p["w_g"], p["nq"], p["nk1"], p["nk2"]]
    return pl.pallas_call(
        _inproj_kernel,
        grid=(m // tm,),
        in_specs=[pl.BlockSpec((tm, d), lambda i: (i, 0))] + [wspec(w) for w in ws],
        out_specs=[o[0] for o in outs],
        out_shape=[o[1] for o in outs],
        compiler_params=_cparams(("parallel",)),
        name="mixer_inproj",
    )(x, *ws)


CONV_HALO = 32
CONV_CHUNK = 64


def _conv_p_kernel(a_ref, ah_ref, cx_ref, ch_ref, bg_ref, wa_ref, ba_ref, ga_ref, wb_ref,
                   ca_ref, yb_ref, xa_ref, xb_ref):
    first = pl.program_id(1) == 0
    tc = a_ref.shape[0]
    xa_ref[0:CONV_HALO, :] = jnp.where(first, 0.0, ah_ref[...])
    xa_ref[CONV_HALO:, :] = a_ref[...]
    xb_ref[0:CONV_HALO, :] = jnp.where(first, 0.0, ch_ref[...])
    xb_ref[CONV_HALO:, :] = cx_ref[...]
    for c in range(tc // CONV_CHUNK):
        r0 = c * CONV_CHUNK
        acc = None
        for k in range(CONV_A):
            o = r0 + CONV_HALO - (CONV_A - 1) + k
            term = wa_ref[k:k + 1, :] * xa_ref[o:o + CONV_CHUNK, :]
            acc = term if acc is None else acc + term
        y = _rms(acc + ba_ref[...], ga_ref[...])
        ca_ref[r0:r0 + CONV_CHUNK, :] = (y * _sigmoid(y)).astype(BF16)
        acc = None
        for k in range(CONV_B):
            o = r0 + CONV_HALO - (CONV_B - 1) + k
            term = wb_ref[k:k + 1, :] * xb_ref[o:o + CONV_CHUNK, :]
            acc = term if acc is None else acc + term
        yb_ref[r0:r0 + CONV_CHUNK, :] = (bg_ref[r0:r0 + CONV_CHUNK, :] * acc).astype(BF16)


def _conv_prompt(a, cx, bg, p, l, batch, seq):
    tc = 256
    nt = seq // tc
    hb = tc // CONV_HALO

    def cur(b, i):
        return (b * nt + i, 0)

    def halo(b, i):
        return (jnp.maximum((b * nt + i) * hb - 1, 0), 0)

    def wspec(w):
        return pl.BlockSpec((None,) + w.shape[1:], lambda b, i: (l, 0, 0))

    ws = [p["w_conv_a"], p["b_conv_a"], p["norm_conv_a"], p["w_conv_b"]]
    m = batch * seq
    return pl.pallas_call(
        _conv_p_kernel,
        grid=(batch, nt),
        in_specs=[pl.BlockSpec((tc, D_A), cur), pl.BlockSpec((CONV_HALO, D_A), halo),
                  pl.BlockSpec((tc, D_B), cur), pl.BlockSpec((CONV_HALO, D_B), halo),
                  pl.BlockSpec((tc, D_B), cur)] + [wspec(w) for w in ws],
        out_specs=[pl.BlockSpec((tc, D_A), cur), pl.BlockSpec((tc, D_B), cur)],
        out_shape=[jax.ShapeDtypeStruct((m, D_A), BF16), jax.ShapeDtypeStruct((m, D_B), BF16)],
        scratch_shapes=[pltpu.VMEM((tc + CONV_HALO, D_A), F32), pltpu.VMEM((tc + CONV_HALO, D_B), F32)],
        compiler_params=_cparams(("parallel", "arbitrary")),
        name="conv_prompt",
    )(a, a, cx, cx, bg, *ws)


def _conv_s_kernel(bufa_ref, a_ref, bufb_ref, cx_ref, bg_ref, wa_ref, ba_ref, ga_ref, wb_ref,
                   ca_ref, yb_ref, na_ref, nb_ref, *, t_new):
    def col(j, w):
        return slice(j * w, (j + 1) * w)

    def up_a(j):
        return bufa_ref[:, col(j, D_A)] if j < CONV_A - 1 else a_ref[:, col(j - (CONV_A - 1), D_A)]

    def up_b(j):
        return bufb_ref[:, col(j, D_B)] if j < CONV_B - 1 else cx_ref[:, col(j - (CONV_B - 1), D_B)]

    for t in range(t_new):
        acc = None
        for k in range(CONV_A):
            term = wa_ref[k:k + 1, :] * up_a(t + k)
            acc = term if acc is None else acc + term
        y = _rms(acc + ba_ref[...], ga_ref[...])
        ca_ref[:, col(t, D_A)] = (y * _sigmoid(y)).astype(BF16)
        acc = None
        for k in range(CONV_B):
            term = wb_ref[k:k + 1, :] * up_b(t + k)
            acc = term if acc is None else acc + term
        yb_ref[:, col(t, D_B)] = (bg_ref[:, col(t, D_B)] * acc).astype(BF16)
    for j in range(CONV_A - 1):
        na_ref[:, col(j, D_A)] = up_a(j + t_new)
    for j in range(CONV_B - 1):
        nb_ref[:, col(j, D_B)] = up_b(j + t_new)


def _conv_sample(bufa, a, bufb, cx, bg, p, l, t_new):
    ns = a.shape[0]
    tn = min(64, ns)

    def rows(n):
        return pl.BlockSpec((tn, n), lambda i: (i, 0))

    def wspec(w):
        return pl.BlockSpec((None,) + w.shape[1:], lambda i: (l, 0, 0))

    ws = [p["w_conv_a"], p["b_conv_a"], p["norm_conv_a"], p["w_conv_b"]]
    ins = [bufa, a, bufb, cx, bg]
    return pl.pallas_call(
        functools.partial(_conv_s_kernel, t_new=t_new),
        grid=(ns // tn,),
        in_specs=[rows(x.shape[1]) for x in ins] + [wspec(w) for w in ws],
        out_specs=[rows(a.shape[1]), rows(cx.shape[1]), rows(bufa.shape[1]), rows(bufb.shape[1])],
        out_shape=[jax.ShapeDtypeStruct(a.shape, BF16), jax.ShapeDtypeStruct(cx.shape, BF16),
                   jax.ShapeDtypeStruct(bufa.shape, F32), jax.ShapeDtypeStruct(bufb.shape, F32)],
        compiler_params=_cparams(("parallel",)),
        name="conv_sample",
    )(*ins, *ws)


N_COL_BLOCKS = 2 * KV_W // LANES


def _compress_cols(x_ref, r_ref, pe_ref, nk_ref, lhs_ref, z_ref, n_chunks, is_k):
    z = None
    for m in range(D_CMP):
        lhs_ref[0:n_chunks, :] = x_ref[pl.ds(m, n_chunks, stride=D_CMP), :].astype(BF16)
        lhs_ref[n_chunks:, :] = pe_ref[m]
        zz = _dot(lhs_ref[...], r_ref[m])
        z = zz if z is None else z + zz
    z_ref[...] = z
    pe_bias = z_ref[n_chunks:n_chunks + 1, 0:LANES] + z_ref[n_chunks + 1:n_chunks + 2, LANES:]
    comp = z_ref[0:n_chunks, 0:LANES] + z_ref[1:n_chunks + 1, LANES:] + pe_bias
    return jnp.where(is_k, _head_rms(comp, nk_ref[...]), comp)


def _compress_p_kernel(x_ref, r_ref, pe_ref, nk_ref, o_ref, lhs_ref, z_ref):
    n_chunks = o_ref.shape[0]
    is_k = pl.program_id(1) < N_COL_BLOCKS // 2
    o_ref[...] = _compress_cols(x_ref, r_ref, pe_ref, nk_ref, lhs_ref, z_ref, n_chunks, is_k).astype(BF16)


def _compress_prompt(kvc, p, l, batch, seq):
    nck = seq // D_CMP
    half = N_COL_BLOCKS // 2
    return pl.pallas_call(
        _compress_p_kernel,
        grid=(batch, N_COL_BLOCKS),
        in_specs=[pl.BlockSpec((seq, LANES), lambda b, c: (b, c)),
                  pl.BlockSpec((None, None, D_CMP, LANES, 2 * LANES), lambda b, c: (l, c // half, 0, 0, 0)),
                  pl.BlockSpec((None, None, D_CMP, 16, LANES), lambda b, c: (l, c // half, 0, 0, 0)),
                  pl.BlockSpec((None, 1, LANES), lambda b, c: (l, 0, 0))],
        out_specs=pl.BlockSpec((None, nck, LANES), lambda b, c: (b, 0, c)),
        out_shape=jax.ShapeDtypeStruct((batch, nck, 2 * KV_W), BF16),
        scratch_shapes=[pltpu.VMEM((nck + 16, LANES), BF16), pltpu.VMEM((nck + 16, 2 * LANES), F32)],
        compiler_params=_cparams(("parallel", "arbitrary")),
        name="compress_prompt",
    )(kvc, p["r_phi"], p["pe16"], p["nk0"])


def _compress_s_kernel(pt_ref, cache_ref, r_ref, pe_ref, nk_ref, o_ref, x_ref, lhs_ref, z_ref, sem,
                       *, layer, n_pages, n_samples):
    j = pl.program_id(0)

    def page_copy(s, pg, c):
        page = pt_ref[(j * n_samples + s) * n_pages + pg]
        src = cache_ref.at[layer, page, :, pl.ds(c * LANES, LANES)]
        dst = x_ref.at[c, pl.ds((s * n_pages + pg) * PAGE_SIZE, PAGE_SIZE), :]
        return pltpu.make_async_copy(src, dst, sem.at[0])

    copies = [(s, pg, c) for s in range(n_samples) for pg in range(n_pages) for c in range(N_COL_BLOCKS)]
    for a in copies:
        page_copy(*a).start()
    for a in copies:
        page_copy(*a).wait()
    n_chunks = o_ref.shape[0]
    half = N_COL_BLOCKS // 2
    for c in range(N_COL_BLOCKS):
        o_ref[:, c * LANES:(c + 1) * LANES] = _compress_cols(
            x_ref.at[c], r_ref.at[c // half], pe_ref.at[c // half], nk_ref, lhs_ref, z_ref,
            n_chunks, c < half).astype(BF16)


def _compress_sample(cache, page_table, p, l):
    ns, n_pages = page_table.shape
    spb = min(SAMPLES_PER_COMPRESS_STEP, ns)
    rows = spb * n_pages * PAGE_SIZE
    nck = rows // D_CMP
    grid_spec = pltpu.PrefetchScalarGridSpec(
        num_scalar_prefetch=1,
        grid=(ns // spb,),
        in_specs=[pl.BlockSpec(memory_space=pl.ANY),
                  pl.BlockSpec((None, 2, D_CMP, LANES, 2 * LANES), lambda j, pt: (l, 0, 0, 0, 0)),
                  pl.BlockSpec((None, 2, D_CMP, 16, LANES), lambda j, pt: (l, 0, 0, 0, 0)),
                  pl.BlockSpec((None, 1, LANES), lambda j, pt: (l, 0, 0))],
        out_specs=pl.BlockSpec((nck, 2 * KV_W), lambda j, pt: (j, 0)),
        scratch_shapes=[pltpu.VMEM((N_COL_BLOCKS, rows, LANES), F32), pltpu.VMEM((nck + 16, LANES), BF16),
                        pltpu.VMEM((nck + 16, 2 * LANES), F32), pltpu.SemaphoreType.DMA((1,))],
    )
    return pl.pallas_call(
        functools.partial(_compress_s_kernel, layer=l, n_pages=n_pages, n_samples=spb),
        grid_spec=grid_spec,
        out_shape=jax.ShapeDtypeStruct((ns * n_pages * PAGE_SIZE // D_CMP, 2 * KV_W), BF16),
        compiler_params=_cparams(("arbitrary",)),
        name="compress_sample",
    )(page_table.reshape(-1), cache, p["r_phi"], p["pe16"], p["nk0"])


def _attn_p_kernel(q_ref, gt_ref, kvc_ref, ov_ref, ks_ref, vs_ref, kw_ref, vw_ref, o_ref,
                   oacc_ref, acc_ref, m_ref, l_ref, selb_ref, *, n_sb):
    tq = q_ref.shape[0]
    ncp = kvc_ref.shape[0]
    t0 = pl.program_id(1) * tq
    tpos = t0 + lax.broadcasted_iota(jnp.int32, (tq, 1), 0)

    def q_head(h):
        return q_ref[:, h * HEAD_DIM:(h + 1) * HEAD_DIM]

    def gate(h, br):
        c = N_BRANCH * h + br
        return gt_ref[:, c:c + 1]


    cpos = lax.broadcasted_iota(jnp.int32, (1, ncp), 1) * D_CMP + (L_CMP - 1)
    cmask = cpos <= tpos
    crel = (cpos - t0).astype(F32)
    blk = lax.broadcasted_iota(jnp.int32, (1, LANES), 1)
    cur = tpos // L_SEL
    valid = blk <= cur
    for g in range(N_KV):
        kcg = kvc_ref[:, g * HEAD_DIM:(g + 1) * HEAD_DIM]
        vcg = kvc_ref[:, KV_W + g * HEAD_DIM:KV_W + (g + 1) * HEAD_DIM]
        psum = None
        for r in range(GROUP):
            h = g * GROUP + r
            s = _dot_nt(q_head(h), kcg) + SLOPES[h] * crel
            s = jnp.where(cmask, s, NEG)
            p = jnp.where(cmask, jnp.exp(s - jnp.max(s, axis=1, keepdims=True)), 0.0)
            den = jnp.sum(p, axis=1, keepdims=True)
            pc = p / jnp.where(den > 0, den, 1.0)
            psum = pc if psum is None else psum + pc
            oacc_ref[h] = gate(h, 0) * _dot(pc.astype(BF16), vcg)
        imp = _forced_importance(_dot_split(psum, ov_ref[...]), blk, cur, n_sb)
        sel = _topk_mask(imp.T, min(N_SEL, n_sb), 0).T
        selb_ref[g] = jnp.where(valid, jnp.where(sel > 0.5, 0.0, NEG), NEG).astype(BF16)

    wk = WINDOW + tq
    start = pl.multiple_of(jnp.maximum(t0 - WINDOW, 0), tq)
    wpos = start + lax.broadcasted_iota(jnp.int32, (1, wk), 1)
    wdist = tpos - wpos
    wbias = jnp.where(wdist >= 0, jnp.where(wdist <= WINDOW, 0.0, NEG), NEG)
    wrel = (wpos - t0).astype(F32)
    kwin = kw_ref[pl.ds(start, wk), :]
    vwin = vw_ref[pl.ds(start, wk), :]
    for h in range(N_HEADS):
        g = h // GROUP
        s = _dot_nt(q_head(h), kwin[:, g * HEAD_DIM:(g + 1) * HEAD_DIM]) + (wbias + SLOPES[h] * wrel)
        p = jnp.exp(s - jnp.max(s, axis=1, keepdims=True))
        den = jnp.sum(p, axis=1, keepdims=True)
        ow = _dot(p.astype(BF16), vwin[:, g * HEAD_DIM:(g + 1) * HEAD_DIM]) / den
        oacc_ref[h] += gate(h, 2) * ow

    m_ref[...] = jnp.full(m_ref.shape, -1e29, F32)
    l_ref[...] = jnp.zeros(l_ref.shape, F32)
    acc_ref[...] = jnp.zeros(acc_ref.shape, F32)
    tk = SEL_TILE
    jrow = lax.broadcasted_iota(jnp.int32, (LANES, 1), 0)

    def sweep(kt, carry):
        k0 = pl.multiple_of(kt * tk, tk)
        kpos = k0 + lax.broadcasted_iota(jnp.int32, (1, tk), 1)
        causal = jnp.where(kpos <= tpos, 0.0, NEG)
        krel = (kpos - t0).astype(F32)
        expand = jnp.where(jrow == kpos // L_SEL, 1.0, 0.0).astype(BF16)
        kblk = ks_ref[pl.ds(k0, tk), :]
        vblk = vs_ref[pl.ds(k0, tk), :]
        for g in range(N_KV):
            bias = _dot(selb_ref[g], expand) + causal
            kg = kblk[:, g * HEAD_DIM:(g + 1) * HEAD_DIM]
            vg = vblk[:, g * HEAD_DIM:(g + 1) * HEAD_DIM]
            for r in range(GROUP):
                h = g * GROUP + r
                s = _dot_nt(q_head(h), kg) + (bias + SLOPES[h] * krel)
                m_prev = m_ref[h]
                m_new = jnp.maximum(m_prev, jnp.max(s, axis=1, keepdims=True))
                p = jnp.exp(s - jnp.concatenate([m_new] * (tk // LANES), axis=1))
                alpha = jnp.exp(m_prev - m_new)
                l_ref[h] = alpha * l_ref[h] + jnp.sum(p, axis=1, keepdims=True)
                acc_ref[h] = acc_ref[h] * alpha[:, :HEAD_DIM] + _dot(p.astype(BF16), vg)
                m_ref[h] = m_new
        return carry

    lax.fori_loop(0, (t0 + tq + tk - 1) // tk, sweep, 0)
    for h in range(N_HEADS):
        den = l_ref[h][:, :HEAD_DIM]
        o_sel = acc_ref[h] / jnp.where(den > 0, den, 1.0)
        o_ref[:, h * HEAD_DIM:(h + 1) * HEAD_DIM] = (oacc_ref[h] + gate(h, 1) * o_sel).astype(o_ref.dtype)


def _attn_prompt(q, gates, kvc, ov, ks, vs, kw, vw, batch, seq):
    tq = Q_BLOCK
    nqb = seq // tq
    nck = seq // D_CMP
    n_sb = seq // L_SEL

    def qrow(b, i):
        return (b * nqb + i, 0)

    def per_batch(b, i):
        return (b, 0)

    resident = pl.BlockSpec((seq, KV_W), per_batch, pipeline_mode=pl.Buffered(1))
    return pl.pallas_call(
        functools.partial(_attn_p_kernel, n_sb=n_sb),
        grid=(batch, nqb),
        in_specs=[pl.BlockSpec((tq, N_HEADS * HEAD_DIM), qrow),
                  pl.BlockSpec((tq, LANES), qrow),
                  pl.BlockSpec((None, nck, 2 * KV_W), lambda b, i: (b, 0, 0)),
                  pl.BlockSpec((nck, LANES), lambda b, i: (0, 0)),
                  resident, resident, resident, resident],
        out_specs=pl.BlockSpec((tq, N_HEADS * HEAD_DIM), qrow),
        out_shape=jax.ShapeDtypeStruct((batch * seq, N_HEADS * HEAD_DIM), BF16),
        scratch_shapes=[pltpu.VMEM((N_HEADS, tq, HEAD_DIM), F32), pltpu.VMEM((N_HEADS, tq, HEAD_DIM), F32),
                        pltpu.VMEM((N_HEADS, tq, LANES), F32), pltpu.VMEM((N_HEADS, tq, LANES), F32),
                        pltpu.VMEM((N_KV, tq, LANES), BF16)],
        compiler_params=_cparams(("parallel", "arbitrary")),
        name="nsa_prompt",
    )(q, gates, kvc, ov, ks, vs, kw, vw)


def _attn_s_kernel(pt_ref, q_ref, gt_ref, kvc_ref, ov_ref, ex_ref, kvs_ref, kvw_ref, win_ref, cache_ref,
                   o_ref, nwin_ref, ksel_ref, kwin_ref, imp_ref, sem, *, layer, n_pages, t_new, n_sb):
    n = pl.program_id(0)
    past = n_pages * PAGE_SIZE
    w_buf = win_ref.shape[1]
    rows = GROUP * t_new

    def page_copy(pg):
        dst = ksel_ref.at[pl.ds(pg * PAGE_SIZE, PAGE_SIZE), :]
        return pltpu.make_async_copy(cache_ref.at[layer, pt_ref[n * n_pages + pg]], dst, sem.at[0])

    for pg in range(n_pages):
        page_copy(pg).start()

    kwin_ref[0:w_buf, :] = win_ref[0]
    kwin_ref[w_buf:w_buf + t_new, :] = kvw_ref[0]
    kwin_ref[w_buf + t_new:, :] = jnp.zeros((kwin_ref.shape[0] - w_buf - t_new, 2 * KV_W), F32)
    nwin_ref[0, 0:w_buf - t_new, :] = win_ref[0, t_new:w_buf, :]
    nwin_ref[0, w_buf - t_new:, :] = kvw_ref[0]

    ridx = lax.broadcasted_iota(jnp.int32, (rows, 1), 0)
    tpos = past + ridx % t_new
    slope = jnp.zeros((rows, 1), F32)

    def head_slopes(g):
        out = slope
        for r in range(GROUP):
            out = jnp.where(ridx // t_new == r, SLOPES[g * GROUP + r], out)
        return out

    def attend(qg, k, v, kpos, mask, sl):
        dist = tpos - kpos
        s = _dot_nt(qg, k) - sl * dist.astype(F32)
        s = jnp.where(mask, s, NEG)
        p = jnp.where(mask, jnp.exp(s - jnp.max(s, axis=1, keepdims=True)), 0.0)
        den = jnp.sum(p, axis=1, keepdims=True)
        pn = p / jnp.where(den > 0, den, 1.0)
        return pn, _dot(pn.astype(BF16), v)

    ncp = kvc_ref.shape[0]
    cpos = lax.broadcasted_iota(jnp.int32, (1, ncp), 1) * D_CMP + (L_CMP - 1)
    o_cmp = []
    for g in range(N_KV):
        kcg = kvc_ref[:, g * HEAD_DIM:(g + 1) * HEAD_DIM]
        vcg = kvc_ref[:, KV_W + g * HEAD_DIM:KV_W + (g + 1) * HEAD_DIM]
        pc, oc = attend(q_ref[0, g], kcg, vcg, cpos, cpos <= tpos, head_slopes(g))
        o_cmp.append(oc)
        imp_rows = _dot_split(pc, ov_ref[...])
        imp = imp_rows[0:t_new]
        for r in range(1, GROUP):
            imp = imp + imp_rows[r * t_new:(r + 1) * t_new]
        imp_ref[g * t_new:(g + 1) * t_new, :] = imp

    gt_rows = N_KV * t_new
    tsel = past + lax.broadcasted_iota(jnp.int32, (gt_rows, 1), 0) % t_new
    blk = lax.broadcasted_iota(jnp.int32, (1, LANES), 1)
    cur = tsel // L_SEL
    valid = blk <= cur
    imp = _forced_importance(imp_ref[...], blk, cur, n_sb)
    sel = _topk_mask(imp, min(N_SEL, n_sb), 1)
    selb = jnp.where(valid, jnp.where(sel > 0.5, 0.0, NEG), NEG).astype(BF16)
    key_bias = _dot(selb, ex_ref[...]).astype(BF16)

    n_w = kwin_ref.shape[0]
    wpos = past - w_buf + lax.broadcasted_iota(jnp.int32, (1, n_w), 1)
    wdist = tpos - wpos
    wmask = (wdist >= 0) & (wdist <= WINDOW)
    o_win = []
    for g in range(N_KV):
        kg = kwin_ref[:, g * HEAD_DIM:(g + 1) * HEAD_DIM].astype(BF16)
        vg = kwin_ref[:, KV_W + g * HEAD_DIM:KV_W + (g + 1) * HEAD_DIM].astype(BF16)
        o_win.append(attend(q_ref[0, g], kg, vg, wpos, wmask, head_slopes(g))[1])

    for pg in range(n_pages):
        page_copy(pg).wait()
    ksel_ref[past:past + t_new, :] = kvs_ref[0]
    ksel_ref[past + t_new:, :] = jnp.zeros((ksel_ref.shape[0] - past - t_new, 2 * KV_W), F32)
    n_k = ksel_ref.shape[0]
    kpos = lax.broadcasted_iota(jnp.int32, (1, n_k), 1)
    col = lax.broadcasted_iota(jnp.int32, (rows, gt_rows), 1)
    for g in range(N_KV):
        spread = jnp.where(col == g * t_new + ridx % t_new, 1.0, 0.0).astype(BF16)
        bias = _dot(spread, key_bias)
        mask = (bias > 0.5 * NEG) & (kpos <= tpos)
        kg = ksel_ref[:, g * HEAD_DIM:(g + 1) * HEAD_DIM].astype(BF16)
        vg = ksel_ref[:, KV_W + g * HEAD_DIM:KV_W + (g + 1) * HEAD_DIM].astype(BF16)
        o_sel = attend(q_ref[0, g], kg, vg, kpos, mask, head_slopes(g))[1]
        gates = gt_ref[0, g]
        o_ref[0, g] = gates[:, 0:1] * o_cmp[g] + gates[:, 1:2] * o_sel + gates[:, 2:3] * o_win[g]


def _attn_sample(q, gates, kvc, ov, expand, kvs_new, kvw_new, win, cache, page_table, l, t_new):
    ns, n_pages = page_table.shape
    past = n_pages * PAGE_SIZE
    w_buf = win.shape[2]
    n_sb = -(-(past + t_new) // L_SEL)
    n_k = n_sb * L_SEL
    n_w = -(-(w_buf + t_new) // 8) * 8
    ncp = past // D_CMP
    rows = GROUP * t_new

    def per_sample(*blk):
        return pl.BlockSpec((1,) + blk, lambda n, pt: (n,) + (0,) * len(blk))

    grid_spec = pltpu.PrefetchScalarGridSpec(
        num_scalar_prefetch=1,
        grid=(ns,),
        in_specs=[per_sample(N_KV, rows, HEAD_DIM), per_sample(N_KV, rows, N_BRANCH),
                  pl.BlockSpec((ncp, 2 * KV_W), lambda n, pt: (n, 0)),
                  pl.BlockSpec(ov.shape, lambda n, pt: (0, 0)),
                  pl.BlockSpec(expand.shape, lambda n, pt: (0, 0)),
                  per_sample(t_new, 2 * KV_W), per_sample(t_new, 2 * KV_W),
                  pl.BlockSpec((None, 1, w_buf, 2 * KV_W), lambda n, pt: (l, n, 0, 0)),
                  pl.BlockSpec(memory_space=pl.ANY)],
        out_specs=[per_sample(N_KV, rows, HEAD_DIM), per_sample(w_buf, 2 * KV_W)],
        scratch_shapes=[pltpu.VMEM((n_k, 2 * KV_W), F32), pltpu.VMEM((n_w, 2 * KV_W), F32),
                        pltpu.VMEM((N_KV * t_new, LANES), F32), pltpu.SemaphoreType.DMA((1,))],
    )
    return pl.pallas_call(
        functools.partial(_attn_s_kernel, layer=l, n_pages=n_pages, t_new=t_new, n_sb=n_sb),
        grid_spec=grid_spec,
        out_shape=[jax.ShapeDtypeStruct((ns, N_KV, rows, HEAD_DIM), F32),
                   jax.ShapeDtypeStruct((ns, w_buf, 2 * KV_W), F32)],
        compiler_params=_cparams(("arbitrary",)),
        name="nsa_sample",
    )(page_table.reshape(-1), q, gates, kvc, ov, expand, kvs_new, kvw_new, win, cache)


def _merge_kernel(x_ref, gm_ref, ca_ref, yb_ref, oc_ref, wm_ref, woa_ref, wob_ref, woc_ref, wo_ref, o_ref):
    x = x_ref[...]
    d = x.shape[1]
    h = _rms(x, gm_ref[...]).astype(BF16)
    mix = _sigmoid(_dot(h, wm_ref[:, 0:d])) * _dot(ca_ref[...], woa_ref[...])
    mix += _sigmoid(_dot(h, wm_ref[:, d:2 * d])) * _dot(yb_ref[...], wob_ref[...])
    mix += _sigmoid(_dot(h, wm_ref[:, 2 * d:])) * _dot(oc_ref[...], woc_ref[...])
    o_ref[...] = x + _dot(mix.astype(BF16), wo_ref[...])


def _merge(x, ca, yb, oc, p, l):
    m, d = x.shape
    tm = min(256, m)

    def wspec(w):
        return pl.BlockSpec((None,) + w.shape[1:], lambda i: (l, 0, 0))

    def rows(a):
        return pl.BlockSpec((tm, a.shape[1]), lambda i: (i, 0))

    ws = [p["w_m"], p["w_out_a"], p["w_out_b"], p["w_out_c"], p["w_o"]]
    return pl.pallas_call(
        _merge_kernel,
        grid=(m // tm,),
        in_specs=[rows(x), wspec(p["norm_mix"]), rows(ca), rows(yb), rows(oc)] + [wspec(w) for w in ws],
        out_specs=rows(x),
        out_shape=jax.ShapeDtypeStruct((m, d), F32),
        compiler_params=_cparams(("parallel",)),
        name="mixer_merge",
    )(x, p["norm_mix"], ca, yb, oc, *ws)


def _prepare_params(norm_ffn, w_ffn_in, w_ffn_out, norm_mix, w_in, w_conv_a, b_conv_a, norm_conv_a,
                    w_out_a, w_conv_b, w_out_b, norm_q, norm_k, pe_cmp, w_phi, w_out_c, w_o):
    depth, d = norm_mix.shape
    sizes = (2 * D_A, 3 * D_B, N_HEADS * HEAD_DIM, 6 * KV_W, N_BRANCH * N_HEADS, N_BRANCH * d)
    offs = [0] + [int(v) for v in np.cumsum(sizes)]
    seg = [w_in[:, :, offs[i]:offs[i + 1]].astype(BF16) for i in range(6)]
    w_g = jnp.pad(seg[4], ((0, 0), (0, 0), (0, LANES - sizes[4])))
    gpb = LANES // HEAD_DIM
    w = w_phi.reshape(depth, 2, L_CMP // D_CMP, D_CMP, HEAD_DIM, HEAD_DIM)
    r_phi = jnp.einsum("leimdf,gh->lemgdihf", w, jnp.eye(gpb, dtype=w.dtype))
    r_phi = r_phi.reshape(depth, 2, D_CMP, LANES, 2 * LANES).astype(BF16)
    pe = pe_cmp.reshape(depth, L_CMP // D_CMP, D_CMP, 2, HEAD_DIM).transpose(0, 3, 2, 1, 4)
    pe = jnp.tile(pe, (1, 1, 1, 1, gpb))
    pe16 = jnp.pad(pe, ((0, 0), (0, 0), (0, 0), (0, 16 - L_CMP // D_CMP), (0, 0))).astype(BF16)
    return {
        "norm_ffn": norm_ffn.reshape(depth, 2, 1, d),
        "w_ffn_in": w_ffn_in.astype(BF16), "w_ffn_out": w_ffn_out.astype(BF16),
        "norm_mix": norm_mix.reshape(depth, 1, d),
        "w_a": seg[0], "w_b": seg[1], "w_q": seg[2], "w_kv": seg[3], "w_g": w_g, "w_m": seg[5],
        "nq": jnp.tile(norm_q * HEAD_DIM ** -0.5, (1, N_HEADS)).reshape(depth, 1, N_HEADS * HEAD_DIM),
        "nk0": jnp.tile(norm_k[:, 0], (1, LANES // HEAD_DIM)).reshape(depth, 1, LANES),
        "nk1": jnp.tile(norm_k[:, 1], (1, N_KV)).reshape(depth, 1, KV_W),
        "nk2": jnp.tile(norm_k[:, 2], (1, N_KV)).reshape(depth, 1, KV_W),
        "w_conv_a": w_conv_a, "b_conv_a": b_conv_a.reshape(depth, 1, D_A),
        "norm_conv_a": norm_conv_a.reshape(depth, 1, D_A), "w_conv_b": w_conv_b,
        "w_out_a": w_out_a.astype(BF16), "w_out_b": w_out_b.astype(BF16),
        "w_out_c": w_out_c.astype(BF16), "w_o": w_o.astype(BF16),
        "r_phi": r_phi, "pe16": pe16,
    }


def _overlap_matrix(n_chunk_rows, n_sb):
    c = np.arange(n_chunk_rows)[:, None] * D_CMP
    j = np.arange(LANES)[None, :] * L_SEL
    ov = (c < j + L_SEL) & (c + L_CMP > j) & (np.arange(LANES)[None, :] < n_sb)
    ov &= (np.arange(n_chunk_rows)[:, None] < n_chunk_rows - 1)
    return jnp.asarray(ov.astype(np.float32), dtype=BF16)


def _block_expand_matrix(n_keys):
    e = np.arange(LANES)[:, None] == (np.arange(n_keys)[None, :] // L_SEL)
    return jnp.asarray(e.astype(np.float32), dtype=BF16)


def kernel(x_prompt, x_sample, cache_cmp_kv, cache_sel_kv, state_win_kv, state_conv_a, state_conv_b,
           page_table, norm_ffn, w_ffn_in, w_ffn_out, norm_mix, w_in, w_conv_a, b_conv_a, norm_conv_a,
           w_out_a, w_conv_b, w_out_b, norm_q, norm_k, pe_cmp, w_phi, w_out_c, w_o):
    batch, seq, d = x_prompt.shape
    ns, t_new, _ = x_sample.shape
    depth = norm_mix.shape[0]
    n_pool = cache_cmp_kv.shape[1]
    n_pages = page_table.shape[1]
    past = n_pages * PAGE_SIZE
    w_buf = state_win_kv.shape[2]
    assert seq % SEL_TILE == 0 and seq >= WINDOW + Q_BLOCK and past % L_SEL == 0

    p = _prepare_params(norm_ffn, w_ffn_in, w_ffn_out, norm_mix, w_in, w_conv_a, b_conv_a, norm_conv_a,
                        w_out_a, w_conv_b, w_out_b, norm_q, norm_k, pe_cmp, w_phi, w_out_c, w_o)
    cache_cmp = cache_cmp_kv.reshape(depth, n_pool, PAGE_SIZE, 2 * KV_W)
    cache_sel = cache_sel_kv.reshape(depth, n_pool, PAGE_SIZE, 2 * KV_W)
    win_state = state_win_kv.reshape(depth, ns, w_buf, 2 * KV_W)
    ov_p = _overlap_matrix(seq // D_CMP, seq // L_SEL)
    n_sb_s = -(-(past + t_new) // L_SEL)
    ov_s = _overlap_matrix(past // D_CMP, n_sb_s)
    expand_s = _block_expand_matrix(n_sb_s * L_SEL)

    xp = x_prompt.reshape(batch * seq, d)
    xs = x_sample.reshape(ns * t_new, d)
    outs = [[] for _ in range(10)]
    for l in range(depth):
        xp = _ffn(xp, p["norm_ffn"], p["w_ffn_in"], p["w_ffn_out"], l, 0)
        xs = _ffn(xs, p["norm_ffn"], p["w_ffn_in"], p["w_ffn_out"], l, 0)

        a, bg, cx, q, kvc, kvs, kvw, gates, ks, vs, kw, vw = _inproj(xp, p, l)
        ca, yb = _conv_prompt(a, cx, bg, p, l, batch, seq)
        kvc_c = _compress_prompt(kvc, p, l, batch, seq)
        oc = _attn_prompt(q, gates, kvc_c, ov_p, ks, vs, kw, vw, batch, seq)
        xp = _merge(xp, ca, yb, oc, p, l)
        outs[0].append(kvc.reshape(batch, seq, 2, N_KV, HEAD_DIM))
        outs[2].append(kvs.reshape(batch, seq, 2, N_KV, HEAD_DIM))
        outs[4].append(kvw.reshape(batch, seq, 2, N_KV, HEAD_DIM)[:, seq - min(WINDOW, seq):])
        outs[6].append(a.reshape(batch, seq, D_A)[:, seq - (CONV_A - 1):])
        outs[8].append(cx.reshape(batch, seq, D_B)[:, seq - (CONV_B - 1):])

        a, bg, cx, q, kvc, kvs, kvw, gates, _, _, _, _ = _inproj(xs, p, l)
        ca, yb, new_a, new_b = _conv_sample(
            state_conv_a[l].reshape(ns, (CONV_A - 1) * D_A), a.reshape(ns, t_new * D_A),
            state_conv_b[l].reshape(ns, (CONV_B - 1) * D_B), cx.reshape(ns, t_new * D_B),
            bg.reshape(ns, t_new * D_B), p, l, t_new)
        kvc_c = _compress_sample(cache_cmp, page_table, p, l)
        qg = q.reshape(ns, t_new, N_KV, GROUP, HEAD_DIM).transpose(0, 2, 3, 1, 4)
        qg = qg.reshape(ns, N_KV, GROUP * t_new, HEAD_DIM)
        gg = gates[:, :N_BRANCH * N_HEADS].reshape(ns, t_new, N_KV, GROUP, N_BRANCH).transpose(0, 2, 3, 1, 4)
        gg = gg.reshape(ns, N_KV, GROUP * t_new, N_BRANCH)
        og, new_win = _attn_sample(qg, gg, kvc_c, ov_s, expand_s, kvs.reshape(ns, t_new, 2 * KV_W),
                                   kvw.reshape(ns, t_new, 2 * KV_W), win_state, cache_sel, page_table, l, t_new)
        oc = og.reshape(ns, N_KV, GROUP, t_new, HEAD_DIM).transpose(0, 3, 1, 2, 4)
        oc = oc.reshape(ns * t_new, N_HEADS * HEAD_DIM).astype(BF16)
        xs = _merge(xs, ca.reshape(ns * t_new, D_A), yb.reshape(ns * t_new, D_B), oc, p, l)
        outs[1].append(kvc.reshape(ns, t_new, 2, N_KV, HEAD_DIM))
        outs[3].append(kvs.reshape(ns, t_new, 2, N_KV, HEAD_DIM))
        outs[5].append(new_win.reshape(ns, w_buf, 2, N_KV, HEAD_DIM))
        outs[7].append(new_a.reshape(ns, CONV_A - 1, D_A))
        outs[9].append(new_b.reshape(ns, CONV_B - 1, D_B))

        xp = _ffn(xp, p["norm_ffn"], p["w_ffn_in"], p["w_ffn_out"], l, 1)
        xs = _ffn(xs, p["norm_ffn"], p["w_ffn_in"], p["w_ffn_out"], l, 1)

    return (xp.reshape(batch, seq, d), xs.reshape(ns, t_new, d)) + tuple(jnp.stack(o) for o in outs)
```

```python
import functools

import numpy as np
import jax
import jax.numpy as jnp
from jax import lax
from jax.experimental import pallas as pl
from jax.experimental.pallas import tpu as pltpu

F32 = jnp.float32
BF16 = jnp.bfloat16

D_A = 512
CONV_A = 31
D_B = 512
CONV_B = 3
N_HEADS = 16
N_KV = 4
GROUP = N_HEADS // N_KV
HEAD_DIM = 64
KV_W = N_KV * HEAD_DIM
L_CMP = 32
D_CMP = 16
L_SEL = 64
N_SEL = 16
N_LOCAL = 2
WINDOW = 512
Q_BLOCK = 128
N_BRANCH = 3
PAGE_SIZE = 128
EPS = 1e-6
NEG = -1e30
FORCE = 1e4
SLOPES = [float(v) for v in np.exp2(-8.0 * np.arange(1, N_HEADS + 1) / N_HEADS).astype(np.float32)]

LANES = 128
SEL_TILE = 512
N_FF_TILES = 2
SAMPLES_PER_COMPRESS_STEP = 4
VMEM_LIMIT = 52 * 1024 * 1024


def _dot(a, b):
    return jnp.dot(a, b, preferred_element_type=F32)


def _dot_nt(a, b):
    return lax.dot_general(a, b, (((1,), (1,)), ((), ())), preferred_element_type=F32)


def _dot_split(a, b):
    hi = a.astype(BF16)
    lo = (a - hi.astype(F32)).astype(BF16)
    return _dot(hi, b) + _dot(lo, b)


def _sigmoid(x):
    return 1.0 / (1.0 + jnp.exp(-x))


def _rms(x, g):
    r = lax.rsqrt(jnp.mean(x * x, axis=-1, keepdims=True) + EPS)
    return (x * r) * g


def _head_rms(z, g):
    rows, n = z.shape
    lo = lax.broadcasted_iota(jnp.int32, (rows, LANES), 1) < HEAD_DIM
    outs = []
    for c in range(n // LANES):
        zc = z[:, c * LANES:(c + 1) * LANES]
        sq = zc * zc
        s_lo = jnp.sum(jnp.where(lo, sq, 0.0), axis=1, keepdims=True)
        s_hi = jnp.sum(jnp.where(lo, 0.0, sq), axis=1, keepdims=True)
        ms = jnp.where(lo, s_lo, s_hi) * (1.0 / HEAD_DIM)
        outs.append((zc * lax.rsqrt(ms + EPS)) * g[:, c * LANES:(c + 1) * LANES])
    return outs[0] if len(outs) == 1 else jnp.concatenate(outs, axis=1)


def _topk_mask(v, k, axis):
    n = v.shape[axis]
    idx = lax.broadcasted_iota(jnp.int32, v.shape, axis).astype(F32)
    sel = jnp.zeros(v.shape, F32)
    for _ in range(k):
        m = jnp.max(v, axis=axis, keepdims=True)
        cand = jnp.where(v == m, idx, float(n))
        first = jnp.min(cand, axis=axis, keepdims=True)
        pick = idx == first
        sel = jnp.where(pick, 1.0, sel)
        v = jnp.where(pick, -3e38, v)
    return sel


def _forced_importance(imp, blk, cur, n_sb):
    local = jnp.where(blk > cur - N_LOCAL, FORCE, imp)
    imp = jnp.where(blk == 0, FORCE, jnp.where(blk <= cur, local, -1.0))
    return jnp.where(blk < n_sb, imp, -2.0)


def _cparams(sem):
    return pltpu.CompilerParams(dimension_semantics=sem, vmem_limit_bytes=VMEM_LIMIT)


def _ffn_kernel(x_ref, g_ref, wu_ref, wv_ref, wo_ref, o_ref, h_ref, acc_ref):
    f = pl.program_id(1)

    @pl.when(f == 0)
    def _():
        h_ref[...] = _rms(x_ref[...], g_ref[...]).astype(BF16)
        acc_ref[...] = jnp.zeros_like(acc_ref)

    h = h_ref[...]
    u = _dot(h, wu_ref[...])
    v = _dot(h, wv_ref[...])
    act = (u * _sigmoid(u)) * v
    acc_ref[...] += _dot(act.astype(BF16), wo_ref[...])

    @pl.when(f == pl.num_programs(1) - 1)
    def _():
        o_ref[...] = x_ref[...] + 0.5 * acc_ref[...]


def _ffn(x, norm_ffn, w_in, w_out, l, hf):
    m, d = x.shape
    ff = w_out.shape[2]
    tf = ff // N_FF_TILES
    tm = min(512, m)
    return pl.pallas_call(
        _ffn_kernel,
        grid=(m // tm, N_FF_TILES),
        in_specs=[
            pl.BlockSpec((tm, d), lambda i, f: (i, 0)),
            pl.BlockSpec((None, None, 1, d), lambda i, f: (l, hf, 0, 0)),
            pl.BlockSpec((None, None, d, tf), lambda i, f: (l, hf, 0, f)),
            pl.BlockSpec((None, None, d, tf), lambda i, f: (l, hf, 0, N_FF_TILES + f)),
            pl.BlockSpec((None, None, tf, d), lambda i, f: (l, hf, f, 0)),
        ],
        out_specs=pl.BlockSpec((tm, d), lambda i, f: (i, 0)),
        out_shape=jax.ShapeDtypeStruct((m, d), F32),
        scratch_shapes=[pltpu.VMEM((tm, d), BF16), pltpu.VMEM((tm, d), F32)],
        compiler_params=_cparams(("parallel", "arbitrary")),
        name="ffn_half",
    )(x, norm_ffn, w_in, w_in, w_out)


def _pos_columns(pos):
    lane = lax.broadcasted_iota(jnp.int32, (pos.shape[0], LANES), 1)
    hi = ((pos // L_SEL) * L_SEL).astype(F32)
    lo = (pos % L_SEL).astype(F32)
    return jnp.where(lane < HEAD_DIM + 3, hi, lo) * jnp.where(lane < HEAD_DIM, 0.0, jnp.where(lane < HEAD_DIM + 6, 1.0, 0.0))


def _augmented_keys(kn, pos_cols):
    lane = lax.broadcasted_iota(jnp.int32, pos_cols.shape, 1)
    out = []
    for g in range(kn.shape[1] // HEAD_DIM):
        v = kn[:, (g // 2) * LANES:(g // 2 + 1) * LANES]
        if g % 2:
            v = pltpu.roll(v, HEAD_DIM, 1)
        out.append(jnp.where(lane < HEAD_DIM, v, pos_cols).astype(BF16))
    return out


def _inproj_kernel(x_ref, gm_ref, wa_ref, wb_ref, wq_ref, wkv_ref, wg_ref, nq_ref, nk1_ref, nk2_ref,
                   a_ref, bg_ref, cx_ref, kvc_ref, kvs_ref, kvw_ref, *attn_refs, seq):
    tm = x_ref.shape[0]
    h = _rms(x_ref[...], gm_ref[...]).astype(BF16)
    za = _dot(h, wa_ref[...])
    a_ref[...] = za[:, :D_A] * _sigmoid(za[:, D_A:])
    zb = _dot(h, wb_ref[...])
    bg_ref[...] = zb[:, :D_B]
    cx_ref[...] = zb[:, D_B:2 * D_B] * zb[:, 2 * D_B:]
    qn = _head_rms(_dot(h, wq_ref[...]), nq_ref[...])
    zkv = _dot(h, wkv_ref[...])
    kvc_ref[...] = zkv[:, :2 * KV_W]
    ks = _head_rms(zkv[:, 2 * KV_W:3 * KV_W], nk1_ref[...])
    vs = zkv[:, 3 * KV_W:4 * KV_W]
    kvs_ref[:, :KV_W] = ks
    kvs_ref[:, KV_W:] = vs
    kw = _head_rms(zkv[:, 4 * KV_W:5 * KV_W], nk2_ref[...])
    vw = zkv[:, 5 * KV_W:]
    kvw_ref[:, :KV_W] = kw
    kvw_ref[:, KV_W:] = vw
    gates = _sigmoid(_dot(h, wg_ref[...]))
    if seq is None:
        q_ref, gt_ref = attn_refs
        q_ref[...] = qn.astype(BF16)
        gt_ref[...] = gates
        return
    qt_ref, gtt_ref, ksa_ref, vst_ref, kwa_ref, vwt_ref = attn_refs
    qt_ref[...] = qn.T.astype(BF16)
    gtt_ref[...] = gates.T
    pos = (pl.program_id(0) % (seq // tm)) * tm + lax.broadcasted_iota(jnp.int32, (tm, 1), 0)
    pos_cols = _pos_columns(pos)
    lane = lax.broadcasted_iota(jnp.int32, (tm, LANES), 1)
    onehot = jnp.where(lane == pos // L_SEL, 1.0, 0.0).astype(BF16)
    for g, blk in enumerate(_augmented_keys(ks, pos_cols)):
        ksa_ref[:, 2 * g * LANES:(2 * g + 1) * LANES] = blk
        ksa_ref[:, (2 * g + 1) * LANES:(2 * g + 2) * LANES] = onehot
    for g, blk in enumerate(_augmented_keys(kw, pos_cols)):
        kwa_ref[:, g * LANES:(g + 1) * LANES] = blk
    vst_ref[0] = vs.T.astype(BF16)
    vwt_ref[...] = vw.T.astype(BF16)


def _inproj(x, p, l, seq=None):
    m, d = x.shape
    tm = min(256, m)

    def wspec(w):
        return pl.BlockSpec((None,) + w.shape[1:], lambda i: (l, 0, 0))

    def rows(n, dt):
        return pl.BlockSpec((tm, n), lambda i: (i, 0)), jax.ShapeDtypeStruct((m, n), dt)

    def cols(n, dt):
        return pl.BlockSpec((n, tm), lambda i: (0, i)), jax.ShapeDtypeStruct((n, m), dt)

    outs = [rows(D_A, F32), rows(D_B, F32), rows(D_B, F32),
            rows(2 * KV_W, F32), rows(2 * KV_W, F32), rows(2 * KV_W, F32)]
    if seq is None:
        outs += [rows(N_HEADS * HEAD_DIM, BF16), rows(LANES, F32)]
    else:
        assert seq % tm == 0 and seq // L_SEL <= LANES
        outs += [cols(N_HEADS * HEAD_DIM, BF16), cols(LANES, F32), rows(2 * N_KV * LANES, BF16),
                 (pl.BlockSpec((1, KV_W, tm), lambda i: (i, 0, 0)), jax.ShapeDtypeStruct((m // tm, KV_W, tm), BF16)),
                 rows(N_KV * LANES, BF16), cols(KV_W, BF16)]
    ws = [p["norm_mix"], p["w_a"], p["w_b"], p["w_q"], p["w_kv"], p["w_g"], p["nq"], p["nk1"], p["nk2"]]
    return pl.pallas_call(
        functools.partial(_inproj_kernel, seq=seq),
        grid=(m // tm,),
        in_specs=[pl.BlockSpec((tm, d), lambda i: (i, 0))] + [wspec(w) for w in ws],
        out_specs=[o[0] for o in outs],
        out_shape=[o[1] for o in outs],
        compiler_params=_cparams(("parallel",)),
        name="mixer_inproj",
    )(x, *ws)


CONV_HALO = 32
CONV_CHUNK = 64


def _conv_p_kernel(a_ref, ah_ref, cx_ref, ch_ref, bg_ref, wa_ref, ba_ref, ga_ref, wb_ref,
                   ca_ref, yb_ref, xa_ref, xb_ref):
    first = pl.program_id(1) == 0
    tc = a_ref.shape[0]
    xa_ref[0:CONV_HALO, :] = jnp.where(first, 0.0, ah_ref[...])
    xa_ref[CONV_HALO:, :] = a_ref[...]
    xb_ref[0:CONV_HALO, :] = jnp.where(first, 0.0, ch_ref[...])
    xb_ref[CONV_HALO:, :] = cx_ref[...]
    for c in range(tc // CONV_CHUNK):
        r0 = c * CONV_CHUNK
        acc = None
        for k in range(CONV_A):
            o = r0 + CONV_HALO - (CONV_A - 1) + k
            term = wa_ref[k:k + 1, :] * xa_ref[o:o + CONV_CHUNK, :]
            acc = term if acc is None else acc + term
        y = _rms(acc + ba_ref[...], ga_ref[...])
        ca_ref[r0:r0 + CONV_CHUNK, :] = (y * _sigmoid(y)).astype(BF16)
        acc = None
        for k in range(CONV_B):
            o = r0 + CONV_HALO - (CONV_B - 1) + k
            term = wb_ref[k:k + 1, :] * xb_ref[o:o + CONV_CHUNK, :]
            acc = term if acc is None else acc + term
        yb_ref[r0:r0 + CONV_CHUNK, :] = (bg_ref[r0:r0 + CONV_CHUNK, :] * acc).astype(BF16)


def _conv_prompt(a, cx, bg, p, l, batch, seq):
    tc = 256
    nt = seq // tc
    hb = tc // CONV_HALO

    def cur(b, i):
        return (b * nt + i, 0)

    def halo(b, i):
        return (jnp.maximum((b * nt + i) * hb - 1, 0), 0)

    def wspec(w):
        return pl.BlockSpec((None,) + w.shape[1:], lambda b, i: (l, 0, 0))

    ws = [p["w_conv_a"], p["b_conv_a"], p["norm_conv_a"], p["w_conv_b"]]
    m = batch * seq
    return pl.pallas_call(
        _conv_p_kernel,
        grid=(batch, nt),
        in_specs=[pl.BlockSpec((tc, D_A), cur), pl.BlockSpec((CONV_HALO, D_A), halo),
                  pl.BlockSpec((tc, D_B), cur), pl.BlockSpec((CONV_HALO, D_B), halo),
                  pl.BlockSpec((tc, D_B), cur)] + [wspec(w) for w in ws],
        out_specs=[pl.BlockSpec((tc, D_A), cur), pl.BlockSpec((tc, D_B), cur)],
        out_shape=[jax.ShapeDtypeStruct((m, D_A), BF16), jax.ShapeDtypeStruct((m, D_B), BF16)],
        scratch_shapes=[pltpu.VMEM((tc + CONV_HALO, D_A), F32), pltpu.VMEM((tc + CONV_HALO, D_B), F32)],
        compiler_params=_cparams(("parallel", "arbitrary")),
        name="conv_prompt",
    )(a, a, cx, cx, bg, *ws)


def _conv_s_kernel(bufa_ref, a_ref, bufb_ref, cx_ref, bg_ref, wa_ref, ba_ref, ga_ref, wb_ref,
                   ca_ref, yb_ref, na_ref, nb_ref, *, t_new):
    def col(j, w):
        return slice(j * w, (j + 1) * w)

    def up_a(j):
        return bufa_ref[:, col(j, D_A)] if j < CONV_A - 1 else a_ref[:, col(j - (CONV_A - 1), D_A)]

    def up_b(j):
        return bufb_ref[:, col(j, D_B)] if j < CONV_B - 1 else cx_ref[:, col(j - (CONV_B - 1), D_B)]

    for t in range(t_new):
        acc = None
        for k in range(CONV_A):
            term = wa_ref[k:k + 1, :] * up_a(t + k)
            acc = term if acc is None else acc + term
        y = _rms(acc + ba_ref[...], ga_ref[...])
        ca_ref[:, col(t, D_A)] = (y * _sigmoid(y)).astype(BF16)
        acc = None
        for k in range(CONV_B):
            term = wb_ref[k:k + 1, :] * up_b(t + k)
            acc = term if acc is None else acc + term
        yb_ref[:, col(t, D_B)] = (bg_ref[:, col(t, D_B)] * acc).astype(BF16)
    for j in range(CONV_A - 1):
        na_ref[:, col(j, D_A)] = up_a(j + t_new)
    for j in range(CONV_B - 1):
        nb_ref[:, col(j, D_B)] = up_b(j + t_new)


def _conv_sample(bufa, a, bufb, cx, bg, p, l, t_new):
    ns = a.shape[0]
    tn = min(64, ns)

    def rows(n):
        return pl.BlockSpec((tn, n), lambda i: (i, 0))

    def wspec(w):
        return pl.BlockSpec((None,) + w.shape[1:], lambda i: (l, 0, 0))

    ws = [p["w_conv_a"], p["b_conv_a"], p["norm_conv_a"], p["w_conv_b"]]
    ins = [bufa, a, bufb, cx, bg]
    return pl.pallas_call(
        functools.partial(_conv_s_kernel, t_new=t_new),
        grid=(ns // tn,),
        in_specs=[rows(x.shape[1]) for x in ins] + [wspec(w) for w in ws],
        out_specs=[rows(a.shape[1]), rows(cx.shape[1]), rows(bufa.shape[1]), rows(bufb.shape[1])],
        out_shape=[jax.ShapeDtypeStruct(a.shape, BF16), jax.ShapeDtypeStruct(cx.shape, BF16),
                   jax.ShapeDtypeStruct(bufa.shape, F32), jax.ShapeDtypeStruct(bufb.shape, F32)],
        compiler_params=_cparams(("parallel",)),
        name="conv_sample",
    )(*ins, *ws)


N_COL_BLOCKS = 2 * KV_W // LANES


def _compress_cols(x_ref, r_ref, pe_ref, nk_ref, lhs_ref, z_ref, n_chunks, is_k):
    z = None
    for m in range(D_CMP):
        lhs_ref[0:n_chunks, :] = x_ref[pl.ds(m, n_chunks, stride=D_CMP), :].astype(BF16)
        lhs_ref[n_chunks:, :] = pe_ref[m]
        zz = _dot(lhs_ref[...], r_ref[m])
        z = zz if z is None else z + zz
    z_ref[...] = z
    pe_bias = z_ref[n_chunks:n_chunks + 1, 0:LANES] + z_ref[n_chunks + 1:n_chunks + 2, LANES:]
    comp = z_ref[0:n_chunks, 0:LANES] + z_ref[1:n_chunks + 1, LANES:] + pe_bias
    return _head_rms(comp, nk_ref[...]) if is_k else comp


def _compress_p_kernel(x_ref, r_ref, pe_ref, nk_ref, kca_ref, vct_ref, lhs_ref, z_ref):
    n_chunks = kca_ref.shape[0]
    half = N_COL_BLOCKS // 2

    @pl.when(pl.program_id(1) < half)
    def _():
        kn = _compress_cols(x_ref, r_ref, pe_ref, nk_ref, lhs_ref, z_ref, n_chunks, True)
        c_end = lax.broadcasted_iota(jnp.int32, (n_chunks, 1), 0) * D_CMP + (L_CMP - 1)
        for g, blk in enumerate(_augmented_keys(kn, _pos_columns(c_end))):
            kca_ref[:, g * LANES:(g + 1) * LANES] = blk

    @pl.when(pl.program_id(1) >= half)
    def _():
        comp = _compress_cols(x_ref, r_ref, pe_ref, nk_ref, lhs_ref, z_ref, n_chunks, False)
        vct_ref[...] = comp.T.astype(BF16)


def _compress_prompt(kvc, p, l, batch, seq):
    nck = seq // D_CMP
    half = N_COL_BLOCKS // 2
    return pl.pallas_call(
        _compress_p_kernel,
        grid=(batch, N_COL_BLOCKS),
        in_specs=[pl.BlockSpec((seq, LANES), lambda b, c: (b, c)),
                  pl.BlockSpec((None, None, D_CMP, LANES, 2 * LANES), lambda b, c: (l, c // half, 0, 0, 0)),
                  pl.BlockSpec((None, None, D_CMP, 16, LANES), lambda b, c: (l, c // half, 0, 0, 0)),
                  pl.BlockSpec((None, 1, LANES), lambda b, c: (l, 0, 0))],
        out_specs=[pl.BlockSpec((None, nck, 2 * LANES), lambda b, c: (b, 0, jnp.minimum(c, half - 1))),
                   pl.BlockSpec((None, LANES, nck), lambda b, c: (b, jnp.maximum(c - half, 0), 0))],
        out_shape=[jax.ShapeDtypeStruct((batch, nck, N_KV * LANES), BF16),
                   jax.ShapeDtypeStruct((batch, KV_W, nck), BF16)],
        scratch_shapes=[pltpu.VMEM((nck + 16, LANES), BF16), pltpu.VMEM((nck + 16, 2 * LANES), F32)],
        compiler_params=_cparams(("arbitrary", "arbitrary")),
        name="compress_prompt",
    )(kvc, p["r_phi"], p["pe16"], p["nk0"])


def _compress_s_kernel(pt_ref, cache_ref, r_ref, pe_ref, nk_ref, o_ref, x_ref, lhs_ref, z_ref, sem,
                       *, layer, n_pages, n_samples):
    j = pl.program_id(0)

    def page_copy(s, pg, c):
        page = pt_ref[(j * n_samples + s) * n_pages + pg]
        src = cache_ref.at[layer, page, :, pl.ds(c * LANES, LANES)]
        dst = x_ref.at[c, pl.ds((s * n_pages + pg) * PAGE_SIZE, PAGE_SIZE), :]
        return pltpu.make_async_copy(src, dst, sem.at[0])

    copies = [(s, pg, c) for s in range(n_samples) for pg in range(n_pages) for c in range(N_COL_BLOCKS)]
    for a in copies:
        page_copy(*a).start()
    for a in copies:
        page_copy(*a).wait()
    n_chunks = o_ref.shape[0]
    half = N_COL_BLOCKS // 2
    for c in range(N_COL_BLOCKS):
        o_ref[:, c * LANES:(c + 1) * LANES] = _compress_cols(
            x_ref.at[c], r_ref.at[c // half], pe_ref.at[c // half], nk_ref, lhs_ref, z_ref,
            n_chunks, c < half).astype(BF16)


def _compress_sample(cache, page_table, p, l):
    ns, n_pages = page_table.shape
    spb = min(SAMPLES_PER_COMPRESS_STEP, ns)
    rows = spb * n_pages * PAGE_SIZE
    nck = rows // D_CMP
    grid_spec = pltpu.PrefetchScalarGridSpec(
        num_scalar_prefetch=1,
        grid=(ns // spb,),
        in_specs=[pl.BlockSpec(memory_space=pl.ANY),
                  pl.BlockSpec((None, 2, D_CMP, LANES, 2 * LANES), lambda j, pt: (l, 0, 0, 0, 0)),
                  pl.BlockSpec((None, 2, D_CMP, 16, LANES), lambda j, pt: (l, 0, 0, 0, 0)),
                  pl.BlockSpec((None, 1, LANES), lambda j, pt: (l, 0, 0))],
        out_specs=pl.BlockSpec((nck, 2 * KV_W), lambda j, pt: (j, 0)),
        scratch_shapes=[pltpu.VMEM((N_COL_BLOCKS, rows, LANES), F32), pltpu.VMEM((nck + 16, LANES), BF16),
                        pltpu.VMEM((nck + 16, 2 * LANES), F32), pltpu.SemaphoreType.DMA((1,))],
    )
    return pl.pallas_call(
        functools.partial(_compress_s_kernel, layer=l, n_pages=n_pages, n_samples=spb),
        grid_spec=grid_spec,
        out_shape=jax.ShapeDtypeStruct((ns * n_pages * PAGE_SIZE // D_CMP, 2 * KV_W), BF16),
        compiler_params=_cparams(("arbitrary",)),
        name="compress_sample",
    )(page_table.reshape(-1), cache, p["r_phi"], p["pe16"], p["nk0"])


N_WIN_BLOCKS = WINDOW // Q_BLOCK + 1
Q_AUG_ROWS = 2 * LANES
SLOPE_ROWS = 16


def _attn_p_kernel(qt_ref, gtt_ref, kca_ref, vct_ref, ovt_ref, srow_ref, ksa_ref, vst_ref, *rest, n_sb):
    kw_refs = rest[:N_WIN_BLOCKS]
    vw_refs = rest[N_WIN_BLOCKS:2 * N_WIN_BLOCKS]
    o_ref, qa_ref, acc_ref, m_ref, l_ref, ocw_ref, flag_ref = rest[2 * N_WIN_BLOCKS:]
    tq = qt_ref.shape[1]
    cols = GROUP * tq
    ncp = kca_ref.shape[0]
    n_tiles = ksa_ref.shape[0] // SEL_TILE
    qi = pl.program_id(1)
    t0 = qi * tq
    tok = t0 + lax.broadcasted_iota(jnp.int32, (1, tq), 1)
    tok4 = jnp.concatenate([tok] * GROUP, axis=1)

    def gate_row(g, br):
        rows = [N_BRANCH * (g * GROUP + r) + br for r in range(GROUP)]
        return jnp.concatenate([gtt_ref[c:c + 1, :] for c in rows], axis=1)

    @pl.when((pl.program_id(0) == 0) & (qi == 0))
    def _():
        for g in range(N_KV):
            qa_ref[g, HEAD_DIM:HEAD_DIM + SLOPE_ROWS, :] = srow_ref[g]
            qa_ref[g, HEAD_DIM + SLOPE_ROWS:LANES, :] = jnp.zeros((LANES - HEAD_DIM - SLOPE_ROWS, cols), BF16)

    for g in range(N_KV):
        for r in range(GROUP):
            h = g * GROUP + r
            qa_ref[g, 0:HEAD_DIM, r * tq:(r + 1) * tq] = qt_ref[h * HEAD_DIM:(h + 1) * HEAD_DIM, :]

    cpos = lax.broadcasted_iota(jnp.int32, (ncp, 1), 0) * D_CMP + (L_CMP - 1)
    cmask = cpos <= tok4
    imps = []
    for g in range(N_KV):
        s = _dot(kca_ref[:, g * LANES:(g + 1) * LANES], qa_ref[g, 0:LANES, :])
        s = jnp.where(cmask, s, NEG)
        p = jnp.where(cmask, jnp.exp(s - jnp.max(s, axis=0, keepdims=True)), 0.0)
        den = jnp.sum(p, axis=0, keepdims=True)
        pc = p / jnp.where(den > 0, den, 1.0)
        psum = pc[:, 0:tq]
        for r in range(1, GROUP):
            psum = psum + pc[:, r * tq:(r + 1) * tq]
        hi = psum.astype(BF16)
        lo = (psum - hi.astype(F32)).astype(BF16)
        imps.append(_dot(ovt_ref[...], hi) + _dot(ovt_ref[...], lo))
        ocw_ref[g] = gate_row(g, 0) * _dot(vct_ref[g * HEAD_DIM:(g + 1) * HEAD_DIM, :], pc.astype(BF16))

    blk = lax.broadcasted_iota(jnp.int32, (LANES, 1), 0)
    cur = tok4 // L_SEL
    imp = _forced_importance(jnp.concatenate(imps, axis=1), blk, cur, n_sb)
    sel = jnp.where(blk <= cur, _topk_mask(imp, min(N_SEL, n_sb), 0), 0.0)
    selb = jnp.where(sel > 0.5, 0.0, NEG).astype(BF16)
    blocks_per_tile = SEL_TILE // L_SEL
    for g in range(N_KV):
        qa_ref[g, LANES:, :] = jnp.concatenate([selb[:, g * tq:(g + 1) * tq]] * GROUP, axis=1)
        for kt in range(n_tiles):
            used = jnp.max(sel[kt * blocks_per_tile:(kt + 1) * blocks_per_tile, g * tq:(g + 1) * tq])
            flag_ref[g * n_tiles + kt] = (used > 0.5).astype(jnp.int32)

    wpos = t0 - WINDOW + lax.broadcasted_iota(jnp.int32, (N_WIN_BLOCKS * tq, 1), 0)
    wdist = tok4 - wpos
    wbias = jnp.where(wpos >= 0, jnp.where(wdist >= 0, jnp.where(wdist <= WINDOW, 0.0, NEG), NEG), NEG)
    for g in range(N_KV):
        kwin = jnp.concatenate([kr[:, g * LANES:(g + 1) * LANES] for kr in kw_refs], axis=0)
        vwin = jnp.concatenate([vr[g * HEAD_DIM:(g + 1) * HEAD_DIM, :] for vr in vw_refs], axis=1)
        s = _dot(kwin, qa_ref[g, 0:LANES, :]) + wbias
        p = jnp.exp(s - jnp.max(s, axis=0, keepdims=True))
        den = jnp.sum(p, axis=0, keepdims=True)
        ocw_ref[g] += gate_row(g, 2) * (_dot(vwin, p.astype(BF16)) / den)

    m_ref[...] = jnp.full(m_ref.shape, -1e29, F32)
    l_ref[...] = jnp.zeros(l_ref.shape, F32)
    acc_ref[...] = jnp.zeros(acc_ref.shape, F32)
    half = SEL_TILE // 2

    def tile_update(kt, g, extra):
        k0 = pl.multiple_of(kt * SEL_TILE, SEL_TILE)
        s = _dot(ksa_ref[pl.ds(k0, SEL_TILE), 2 * g * LANES:(2 * g + 2) * LANES], qa_ref[g])
        if extra is not None:
            s = s + extra
        m_prev = m_ref[g]
        m_new = jnp.maximum(m_prev, jnp.max(s, axis=0, keepdims=True))
        p = jnp.exp(s - m_new)
        alpha = jnp.exp(m_prev - m_new)
        l_ref[g] = alpha * l_ref[g] + jnp.sum(p, axis=0, keepdims=True)
        pb = p.astype(BF16)
        pv = _dot(vst_ref[2 * kt, g * HEAD_DIM:(g + 1) * HEAD_DIM, :], pb[0:half])
        pv += _dot(vst_ref[2 * kt + 1, g * HEAD_DIM:(g + 1) * HEAD_DIM, :], pb[half:])
        acc_ref[g] = acc_ref[g] * alpha + pv
        m_ref[g] = m_new

    n_full = t0 // SEL_TILE

    def sweep(kt, carry):
        for g in range(N_KV):
            @pl.when(flag_ref[g * n_tiles + kt] > 0)
            def _():
                tile_update(kt, g, None)
        return carry

    lax.fori_loop(0, n_full, sweep, 0)
    kpos = n_full * SEL_TILE + lax.broadcasted_iota(jnp.int32, (SEL_TILE, 1), 0)
    causal = jnp.where(kpos <= tok4, 0.0, NEG)
    for g in range(N_KV):
        tile_update(n_full, g, causal)

    for g in range(N_KV):
        den = l_ref[g]
        o_t = ocw_ref[g] + gate_row(g, 1) * (acc_ref[g] / jnp.where(den > 0, den, 1.0))
        for pair in range(GROUP // 2):
            two = jnp.concatenate([o_t[:, (2 * pair) * tq:(2 * pair + 1) * tq],
                                   o_t[:, (2 * pair + 1) * tq:(2 * pair + 2) * tq]], axis=0)
            c0 = (g * GROUP + 2 * pair) * HEAD_DIM
            o_ref[:, c0:c0 + 2 * HEAD_DIM] = two.T.astype(o_ref.dtype)


def _attn_prompt(qt, gtt, kca, vct, ovt, srow, ksa, vst, kwa, vwt, batch, seq):
    tq = Q_BLOCK
    nqb = seq // tq
    nck = seq // D_CMP
    n_sb = seq // L_SEL
    chunks = vst.shape[0] // batch
    assert vst.shape[2] * 2 == SEL_TILE and 2 * HEAD_DIM == LANES

    def qcol(b, i):
        return (0, b * nqb + i)

    def win_rows(j):
        return lambda b, i: (b * nqb + jnp.maximum(i - (N_WIN_BLOCKS - 1) + j, 0), 0)

    def win_cols(j):
        return lambda b, i: (0, b * nqb + jnp.maximum(i - (N_WIN_BLOCKS - 1) + j, 0))

    in_specs = [pl.BlockSpec((N_HEADS * HEAD_DIM, tq), qcol),
                pl.BlockSpec((LANES, tq), qcol),
                pl.BlockSpec((None, nck, N_KV * LANES), lambda b, i: (b, 0, 0)),
                pl.BlockSpec((None, KV_W, nck), lambda b, i: (b, 0, 0)),
                pl.BlockSpec(ovt.shape, lambda b, i: (0, 0)),
                pl.BlockSpec(srow.shape, lambda b, i: (0, 0, 0)),
                pl.BlockSpec((seq, 2 * N_KV * LANES), lambda b, i: (b, 0), pipeline_mode=pl.Buffered(1)),
                pl.BlockSpec((chunks, KV_W, vst.shape[2]), lambda b, i: (b, 0, 0), pipeline_mode=pl.Buffered(1))]
    in_specs += [pl.BlockSpec((tq, N_KV * LANES), win_rows(j)) for j in range(N_WIN_BLOCKS)]
    in_specs += [pl.BlockSpec((KV_W, tq), win_cols(j)) for j in range(N_WIN_BLOCKS)]
    cols = GROUP * tq
    return pl.pallas_call(
        functools.partial(_attn_p_kernel, n_sb=n_sb),
        grid=(batch, nqb),
        in_specs=in_specs,
        out_specs=pl.BlockSpec((tq, N_HEADS * HEAD_DIM), lambda b, i: (b * nqb + i, 0)),
        out_shape=jax.ShapeDtypeStruct((batch * seq, N_HEADS * HEAD_DIM), BF16),
        scratch_shapes=[pltpu.VMEM((N_KV, Q_AUG_ROWS, cols), BF16), pltpu.VMEM((N_KV, HEAD_DIM, cols), F32),
                        pltpu.VMEM((N_KV, 1, cols), F32), pltpu.VMEM((N_KV, 1, cols), F32),
                        pltpu.VMEM((N_KV, HEAD_DIM, cols), F32),
                        pltpu.SMEM((N_KV * (seq // SEL_TILE),), jnp.int32)],
        compiler_params=_cparams(("arbitrary", "arbitrary")),
        name="nsa_prompt",
    )(qt, gtt, kca, vct, ovt, srow, ksa, vst, *([kwa] * N_WIN_BLOCKS), *([vwt] * N_WIN_BLOCKS))


SAMPLES_PER_ATTN_STEP = 2


def _attn_s_kernel(pt_ref, q_ref, gt_ref, kvc_ref, ov_ref, ex_ref, kvs_ref, kvw_ref, win_ref, cache_ref,
                   o_ref, nwin_ref, ksel_ref, kwin_ref, imp_ref, sem, *, layer, n_pages, t_new, n_sb):
    spb = q_ref.shape[0]
    n0 = pl.program_id(0) * spb
    past = n_pages * PAGE_SIZE
    w_buf = win_ref.shape[1]
    rows = GROUP * t_new
    gt_rows = N_KV * t_new
    ncp = kvc_ref.shape[0] // spb

    def page_copy(s, pg):
        dst = ksel_ref.at[s, pl.ds(pg * PAGE_SIZE, PAGE_SIZE), :]
        return pltpu.make_async_copy(cache_ref.at[layer, pt_ref[(n0 + s) * n_pages + pg]], dst, sem.at[0])

    for s in range(spb):
        for pg in range(n_pages):
            page_copy(s, pg).start()

    for s in range(spb):
        kwin_ref[s, 0:w_buf, :] = win_ref[s]
        kwin_ref[s, w_buf:w_buf + t_new, :] = kvw_ref[s]
        kwin_ref[s, w_buf + t_new:, :] = jnp.zeros((kwin_ref.shape[1] - w_buf - t_new, 2 * KV_W), F32)
        nwin_ref[s, 0:w_buf - t_new, :] = win_ref[s, t_new:w_buf, :]
        nwin_ref[s, w_buf - t_new:, :] = kvw_ref[s]

    ridx = lax.broadcasted_iota(jnp.int32, (rows, 1), 0)
    tpos = past + ridx % t_new

    def head_slopes(g):
        out = jnp.zeros((rows, 1), F32)
        for r in range(GROUP):
            out = jnp.where(ridx // t_new == r, SLOPES[g * GROUP + r], out)
        return out

    def attend(qg, k, v, kpos, mask, sl):
        dist = tpos - kpos
        s = _dot_nt(qg, k) - sl * dist.astype(F32)
        s = jnp.where(mask, s, NEG)
        p = jnp.where(mask, jnp.exp(s - jnp.max(s, axis=1, keepdims=True)), 0.0)
        den = jnp.sum(p, axis=1, keepdims=True)
        pn = p / jnp.where(den > 0, den, 1.0)
        return pn, _dot(pn.astype(BF16), v)

    cpos = lax.broadcasted_iota(jnp.int32, (1, ncp), 1) * D_CMP + (L_CMP - 1)
    o_cmp = {}
    for s in range(spb):
        for g in range(N_KV):
            kcg = kvc_ref[s * ncp:(s + 1) * ncp, g * HEAD_DIM:(g + 1) * HEAD_DIM]
            vcg = kvc_ref[s * ncp:(s + 1) * ncp, KV_W + g * HEAD_DIM:KV_W + (g + 1) * HEAD_DIM]
            pc, o_cmp[s, g] = attend(q_ref[s, g], kcg, vcg, cpos, cpos <= tpos, head_slopes(g))
            imp_rows = _dot_split(pc, ov_ref[...])
            imp = imp_rows[0:t_new]
            for r in range(1, GROUP):
                imp = imp + imp_rows[r * t_new:(r + 1) * t_new]
            imp_ref[s * gt_rows + g * t_new:s * gt_rows + (g + 1) * t_new, :] = imp

    tsel = past + lax.broadcasted_iota(jnp.int32, (spb * gt_rows, 1), 0) % t_new
    blk = lax.broadcasted_iota(jnp.int32, (1, LANES), 1)
    cur = tsel // L_SEL
    imp = _forced_importance(imp_ref[...], blk, cur, n_sb)
    sel = _topk_mask(imp, min(N_SEL, n_sb), 1)
    selb = jnp.where(blk <= cur, jnp.where(sel > 0.5, 0.0, NEG), NEG).astype(BF16)
    key_bias = _dot(selb, ex_ref[...]).astype(BF16)

    n_w = kwin_ref.shape[1]
    wpos = past - w_buf + lax.broadcasted_iota(jnp.int32, (1, n_w), 1)
    wdist = tpos - wpos
    wmask = (wdist >= 0) & (wdist <= WINDOW)
    o_win = {}
    for s in range(spb):
        for g in range(N_KV):
            kg = kwin_ref[s, :, g * HEAD_DIM:(g + 1) * HEAD_DIM].astype(BF16)
            vg = kwin_ref[s, :, KV_W + g * HEAD_DIM:KV_W + (g + 1) * HEAD_DIM].astype(BF16)
            o_win[s, g] = attend(q_ref[s, g], kg, vg, wpos, wmask, head_slopes(g))[1]

    for s in range(spb):
        for pg in range(n_pages):
            page_copy(s, pg).wait()
    n_k = ksel_ref.shape[1]
    kpos = lax.broadcasted_iota(jnp.int32, (1, n_k), 1)
    col = lax.broadcasted_iota(jnp.int32, (rows, spb * gt_rows), 1)
    for s in range(spb):
        ksel_ref[s, past:past + t_new, :] = kvs_ref[s]
        ksel_ref[s, past + t_new:, :] = jnp.zeros((n_k - past - t_new, 2 * KV_W), F32)
        for g in range(N_KV):
            spread = jnp.where(col == s * gt_rows + g * t_new + ridx % t_new, 1.0, 0.0).astype(BF16)
            bias = _dot(spread, key_bias)
            mask = (bias > 0.5 * NEG) & (kpos <= tpos)
            kg = ksel_ref[s, :, g * HEAD_DIM:(g + 1) * HEAD_DIM].astype(BF16)
            vg = ksel_ref[s, :, KV_W + g * HEAD_DIM:KV_W + (g + 1) * HEAD_DIM].astype(BF16)
            o_sel = attend(q_ref[s, g], kg, vg, kpos, mask, head_slopes(g))[1]
            gates = gt_ref[s, g]
            o_ref[s, g] = gates[:, 0:1] * o_cmp[s, g] + gates[:, 1:2] * o_sel + gates[:, 2:3] * o_win[s, g]


def _attn_sample(q, gates, kvc, ov, expand, kvs_new, kvw_new, win, cache, page_table, l, t_new):
    ns, n_pages = page_table.shape
    spb = min(SAMPLES_PER_ATTN_STEP, ns)
    past = n_pages * PAGE_SIZE
    w_buf = win.shape[2]
    n_sb = -(-(past + t_new) // L_SEL)
    n_k = n_sb * L_SEL
    n_w = -(-(w_buf + t_new) // 8) * 8
    ncp = past // D_CMP
    rows = GROUP * t_new

    def per_sample(*blk):
        return pl.BlockSpec((spb,) + blk, lambda n, pt: (n,) + (0,) * len(blk))

    grid_spec = pltpu.PrefetchScalarGridSpec(
        num_scalar_prefetch=1,
        grid=(ns // spb,),
        in_specs=[per_sample(N_KV, rows, HEAD_DIM), per_sample(N_KV, rows, N_BRANCH),
                  pl.BlockSpec((spb * ncp, 2 * KV_W), lambda n, pt: (n, 0)),
                  pl.BlockSpec(ov.shape, lambda n, pt: (0, 0)),
                  pl.BlockSpec(expand.shape, lambda n, pt: (0, 0)),
                  per_sample(t_new, 2 * KV_W), per_sample(t_new, 2 * KV_W),
                  pl.BlockSpec((None, spb, w_buf, 2 * KV_W), lambda n, pt: (l, n, 0, 0)),
                  pl.BlockSpec(memory_space=pl.ANY)],
        out_specs=[per_sample(N_KV, rows, HEAD_DIM), per_sample(w_buf, 2 * KV_W)],
        scratch_shapes=[pltpu.VMEM((spb, n_k, 2 * KV_W), F32), pltpu.VMEM((spb, n_w, 2 * KV_W), F32),
                        pltpu.VMEM((spb * N_KV * t_new, LANES), F32), pltpu.SemaphoreType.DMA((1,))],
    )
    return pl.pallas_call(
        functools.partial(_attn_s_kernel, layer=l, n_pages=n_pages, t_new=t_new, n_sb=n_sb),
        grid_spec=grid_spec,
        out_shape=[jax.ShapeDtypeStruct((ns, N_KV, rows, HEAD_DIM), F32),
                   jax.ShapeDtypeStruct((ns, w_buf, 2 * KV_W), F32)],
        compiler_params=_cparams(("arbitrary",)),
        name="nsa_sample",
    )(page_table.reshape(-1), q, gates, kvc, ov, expand, kvs_new, kvw_new, win, cache)


def _merge_kernel(x_ref, gm_ref, ca_ref, yb_ref, oc_ref, wm_ref, woa_ref, wob_ref, woc_ref, wo_ref, o_ref):
    x = x_ref[...]
    d = x.shape[1]
    h = _rms(x, gm_ref[...]).astype(BF16)
    mix = _sigmoid(_dot(h, wm_ref[:, 0:d])) * _dot(ca_ref[...], woa_ref[...])
    mix += _sigmoid(_dot(h, wm_ref[:, d:2 * d])) * _dot(yb_ref[...], wob_ref[...])
    mix += _sigmoid(_dot(h, wm_ref[:, 2 * d:])) * _dot(oc_ref[...], woc_ref[...])
    o_ref[...] = x + _dot(mix.astype(BF16), wo_ref[...])


def _merge(x, ca, yb, oc, p, l):
    m, d = x.shape
    tm = min(256, m)

    def wspec(w):
        return pl.BlockSpec((None,) + w.shape[1:], lambda i: (l, 0, 0))

    def rows(a):
        return pl.BlockSpec((tm, a.shape[1]), lambda i: (i, 0))

    ws = [p["w_m"], p["w_out_a"], p["w_out_b"], p["w_out_c"], p["w_o"]]
    return pl.pallas_call(
        _merge_kernel,
        grid=(m // tm,),
        in_specs=[rows(x), wspec(p["norm_mix"]), rows(ca), rows(yb), rows(oc)] + [wspec(w) for w in ws],
        out_specs=rows(x),
        out_shape=jax.ShapeDtypeStruct((m, d), F32),
        compiler_params=_cparams(("parallel",)),
        name="mixer_merge",
    )(x, p["norm_mix"], ca, yb, oc, *ws)


def _prepare_params(norm_ffn, w_ffn_in, w_ffn_out, norm_mix, w_in, w_conv_a, b_conv_a, norm_conv_a,
                    w_out_a, w_conv_b, w_out_b, norm_q, norm_k, pe_cmp, w_phi, w_out_c, w_o):
    depth, d = norm_mix.shape
    sizes = (2 * D_A, 3 * D_B, N_HEADS * HEAD_DIM, 6 * KV_W, N_BRANCH * N_HEADS, N_BRANCH * d)
    offs = [0] + [int(v) for v in np.cumsum(sizes)]
    seg = [w_in[:, :, offs[i]:offs[i + 1]].astype(BF16) for i in range(6)]
    w_g = jnp.pad(seg[4], ((0, 0), (0, 0), (0, LANES - sizes[4])))
    gpb = LANES // HEAD_DIM
    w = w_phi.reshape(depth, 2, L_CMP // D_CMP, D_CMP, HEAD_DIM, HEAD_DIM)
    r_phi = jnp.einsum("leimdf,gh->lemgdihf", w, jnp.eye(gpb, dtype=w.dtype))
    r_phi = r_phi.reshape(depth, 2, D_CMP, LANES, 2 * LANES).astype(BF16)
    pe = pe_cmp.reshape(depth, L_CMP // D_CMP, D_CMP, 2, HEAD_DIM).transpose(0, 3, 2, 1, 4)
    pe = jnp.tile(pe, (1, 1, 1, 1, gpb))
    pe16 = jnp.pad(pe, ((0, 0), (0, 0), (0, 0), (0, 16 - L_CMP // D_CMP), (0, 0))).astype(BF16)
    return {
        "norm_ffn": norm_ffn.reshape(depth, 2, 1, d),
        "w_ffn_in": w_ffn_in.astype(BF16), "w_ffn_out": w_ffn_out.astype(BF16),
        "norm_mix": norm_mix.reshape(depth, 1, d),
        "w_a": seg[0], "w_b": seg[1], "w_q": seg[2], "w_kv": seg[3], "w_g": w_g, "w_m": seg[5],
        "nq": jnp.tile(norm_q * HEAD_DIM ** -0.5, (1, N_HEADS)).reshape(depth, 1, N_HEADS * HEAD_DIM),
        "nk0": jnp.tile(norm_k[:, 0], (1, LANES // HEAD_DIM)).reshape(depth, 1, LANES),
        "nk1": jnp.tile(norm_k[:, 1], (1, N_KV)).reshape(depth, 1, KV_W),
        "nk2": jnp.tile(norm_k[:, 2], (1, N_KV)).reshape(depth, 1, KV_W),
        "w_conv_a": w_conv_a, "b_conv_a": b_conv_a.reshape(depth, 1, D_A),
        "norm_conv_a": norm_conv_a.reshape(depth, 1, D_A), "w_conv_b": w_conv_b,
        "w_out_a": w_out_a.astype(BF16), "w_out_b": w_out_b.astype(BF16),
        "w_out_c": w_out_c.astype(BF16), "w_o": w_o.astype(BF16),
        "r_phi": r_phi, "pe16": pe16,
    }


def _overlap_matrix(n_chunk_rows, n_sb):
    c = np.arange(n_chunk_rows)[:, None] * D_CMP
    j = np.arange(LANES)[None, :] * L_SEL
    ov = (c < j + L_SEL) & (c + L_CMP > j) & (np.arange(LANES)[None, :] < n_sb)
    ov &= (np.arange(n_chunk_rows)[:, None] < n_chunk_rows - 1)
    return jnp.asarray(ov.astype(np.float32), dtype=BF16)


def _block_expand_matrix(n_keys):
    e = np.arange(LANES)[:, None] == (np.arange(n_keys)[None, :] // L_SEL)
    return jnp.asarray(e.astype(np.float32), dtype=BF16)


def _slope_rows():
    rows = np.zeros((N_KV, SLOPE_ROWS, GROUP * Q_BLOCK), np.float32)
    for h, s in enumerate(SLOPES):
        rest = np.float32(s)
        for i in range(3):
            piece = np.float32(rest.astype(jnp.bfloat16))
            rest = np.float32(rest - piece)
            rows[h // GROUP, [i, i + 3], (h % GROUP) * Q_BLOCK:(h % GROUP + 1) * Q_BLOCK] = piece
        assert rest == 0.0
    return jnp.asarray(rows, dtype=BF16)


def kernel(x_prompt, x_sample, cache_cmp_kv, cache_sel_kv, state_win_kv, state_conv_a, state_conv_b,
           page_table, norm_ffn, w_ffn_in, w_ffn_out, norm_mix, w_in, w_conv_a, b_conv_a, norm_conv_a,
           w_out_a, w_conv_b, w_out_b, norm_q, norm_k, pe_cmp, w_phi, w_out_c, w_o):
    batch, seq, d = x_prompt.shape
    ns, t_new, _ = x_sample.shape
    depth = norm_mix.shape[0]
    n_pool = cache_cmp_kv.shape[1]
    n_pages = page_table.shape[1]
    past = n_pages * PAGE_SIZE
    w_buf = state_win_kv.shape[2]
    assert seq % SEL_TILE == 0 and seq >= WINDOW + Q_BLOCK and past % L_SEL == 0

    p = _prepare_params(norm_ffn, w_ffn_in, w_ffn_out, norm_mix, w_in, w_conv_a, b_conv_a, norm_conv_a,
                        w_out_a, w_conv_b, w_out_b, norm_q, norm_k, pe_cmp, w_phi, w_out_c, w_o)
    cache_cmp = cache_cmp_kv.reshape(depth, n_pool, PAGE_SIZE, 2 * KV_W)
    cache_sel = cache_sel_kv.reshape(depth, n_pool, PAGE_SIZE, 2 * KV_W)
    win_state = state_win_kv.reshape(depth, ns, w_buf, 2 * KV_W)
    ovt_p = _overlap_matrix(seq // D_CMP, seq // L_SEL).T
    srow = _slope_rows()
    n_sb_s = -(-(past + t_new) // L_SEL)
    ov_s = _overlap_matrix(past // D_CMP, n_sb_s)
    expand_s = _block_expand_matrix(n_sb_s * L_SEL)

    xp = x_prompt.reshape(batch * seq, d)
    xs = x_sample.reshape(ns * t_new, d)
    outs = [[] for _ in range(10)]
    for l in range(depth):
        xp = _ffn(xp, p["norm_ffn"], p["w_ffn_in"], p["w_ffn_out"], l, 0)
        xs = _ffn(xs, p["norm_ffn"], p["w_ffn_in"], p["w_ffn_out"], l, 0)

        a, bg, cx, kvc, kvs, kvw, qt, gtt, ksa, vst, kwa, vwt = _inproj(xp, p, l, seq)
        ca, yb = _conv_prompt(a, cx, bg, p, l, batch, seq)
        kca, vct = _compress_prompt(kvc, p, l, batch, seq)
        oc = _attn_prompt(qt, gtt, kca, vct, ovt_p, srow, ksa, vst, kwa, vwt, batch, seq)
        xp = _merge(xp, ca, yb, oc, p, l)
        outs[0].append(kvc.reshape(batch, seq, 2, N_KV, HEAD_DIM))
        outs[2].append(kvs.reshape(batch, seq, 2, N_KV, HEAD_DIM))
        outs[4].append(kvw.reshape(batch, seq, 2, N_KV, HEAD_DIM)[:, seq - min(WINDOW, seq):])
        outs[6].append(a.reshape(batch, seq, D_A)[:, seq - (CONV_A - 1):])
        outs[8].append(cx.reshape(batch, seq, D_B)[:, seq - (CONV_B - 1):])

        a, bg, cx, kvc, kvs, kvw, q, gates = _inproj(xs, p, l)
        ca, yb, new_a, new_b = _conv_sample(
            state_conv_a[l].reshape(ns, (CONV_A - 1) * D_A), a.reshape(ns, t_new * D_A),
            state_conv_b[l].reshape(ns, (CONV_B - 1) * D_B), cx.reshape(ns, t_new * D_B),
            bg.reshape(ns, t_new * D_B), p, l, t_new)
        kvc_c = _compress_sample(cache_cmp, page_table, p, l)
        qg = q.reshape(ns, t_new, N_KV, GROUP, HEAD_DIM).transpose(0, 2, 3, 1, 4)
        qg = qg.reshape(ns, N_KV, GROUP * t_new, HEAD_DIM)
        gg = gates[:, :N_BRANCH * N_HEADS].reshape(ns, t_new, N_KV, GROUP, N_BRANCH).transpose(0, 2, 3, 1, 4)
        gg = gg.reshape(ns, N_KV, GROUP * t_new, N_BRANCH)
        og, new_win = _attn_sample(qg, gg, kvc_c, ov_s, expand_s, kvs.reshape(ns, t_new, 2 * KV_W),
                                   kvw.reshape(ns, t_new, 2 * KV_W), win_state, cache_sel, page_table, l, t_new)
        oc = og.reshape(ns, N_KV, GROUP, t_new, HEAD_DIM).transpose(0, 3, 1, 2, 4)
        oc = oc.reshape(ns * t_new, N_HEADS * HEAD_DIM).astype(BF16)
        xs = _merge(xs, ca.reshape(ns * t_new, D_A), yb.reshape(ns * t_new, D_B), oc, p, l)
        outs[1].append(kvc.reshape(ns, t_new, 2, N_KV, HEAD_DIM))
        outs[3].append(kvs.reshape(ns, t_new, 2, N_KV, HEAD_DIM))
        outs[5].append(new_win.reshape(ns, w_buf, 2, N_KV, HEAD_DIM))
        outs[7].append(new_a.reshape(ns, CONV_A - 1, D_A))
        outs[9].append(new_b.reshape(ns, CONV_B - 1, D_B))

        xp = _ffn(xp, p["norm_ffn"], p["w_ffn_in"], p["w_ffn_out"], l, 1)
        xs = _ffn(xs, p["norm_ffn"], p["w_ffn_in"], p["w_ffn_out"], l, 1)

    return (xp.reshape(batch, seq, d), xs.reshape(ns, t_new, d)) + tuple(jnp.stack(o) for o in outs)
```

```python
import functools

import numpy as np
import jax
import jax.numpy as jnp
from jax import lax
from jax.experimental import pallas as pl
from jax.experimental.pallas import tpu as pltpu

F32 = jnp.float32
BF16 = jnp.bfloat16

D_A = 512
CONV_A = 31
D_B = 512
CONV_B = 3
N_HEADS = 16
N_KV = 4
GROUP = N_HEADS // N_KV
HEAD_DIM = 64
KV_W = N_KV * HEAD_DIM
L_CMP = 32
D_CMP = 16
L_SEL = 64
N_SEL = 16
N_LOCAL = 2
WINDOW = 512
Q_BLOCK = 128
N_BRANCH = 3
PAGE_SIZE = 128
EPS = 1e-6
NEG = -1e30
FORCE = 1e4
SLOPES = [float(v) for v in np.exp2(-8.0 * np.arange(1, N_HEADS + 1) / N_HEADS).astype(np.float32)]

LANES = 128
SEL_TILE = 512
N_FF_TILES = 2
SAMPLES_PER_COMPRESS_STEP = 4
VMEM_LIMIT = 52 * 1024 * 1024


def _dot(a, b):
    return jnp.dot(a, b, preferred_element_type=F32)


def _dot_nt(a, b):
    return lax.dot_general(a, b, (((1,), (1,)), ((), ())), preferred_element_type=F32)


def _dot_split(a, b):
    hi = a.astype(BF16)
    lo = (a - hi.astype(F32)).astype(BF16)
    return _dot(hi, b) + _dot(lo, b)


def _sigmoid(x):
    return 1.0 / (1.0 + jnp.exp(-x))


def _rms(x, g):
    r = lax.rsqrt(jnp.mean(x * x, axis=-1, keepdims=True) + EPS)
    return (x * r) * g


def _head_rms(z, g):
    rows, n = z.shape
    lo = lax.broadcasted_iota(jnp.int32, (rows, LANES), 1) < HEAD_DIM
    outs = []
    for c in range(n // LANES):
        zc = z[:, c * LANES:(c + 1) * LANES]
        sq = zc * zc
        s_lo = jnp.sum(jnp.where(lo, sq, 0.0), axis=1, keepdims=True)
        s_hi = jnp.sum(jnp.where(lo, 0.0, sq), axis=1, keepdims=True)
        ms = jnp.where(lo, s_lo, s_hi) * (1.0 / HEAD_DIM)
        outs.append((zc * lax.rsqrt(ms + EPS)) * g[:, c * LANES:(c + 1) * LANES])
    return outs[0] if len(outs) == 1 else jnp.concatenate(outs, axis=1)


def _topk_mask(v, k, axis):
    n = v.shape[axis]
    idx = lax.broadcasted_iota(jnp.int32, v.shape, axis).astype(F32)
    sel = jnp.zeros(v.shape, F32)
    for _ in range(k):
        m = jnp.max(v, axis=axis, keepdims=True)
        cand = jnp.where(v == m, idx, float(n))
        first = jnp.min(cand, axis=axis, keepdims=True)
        pick = idx == first
        sel = jnp.where(pick, 1.0, sel)
        v = jnp.where(pick, -3e38, v)
    return sel


def _forced_importance(imp, blk, cur, n_sb):
    local = jnp.where(blk > cur - N_LOCAL, FORCE, imp)
    imp = jnp.where(blk == 0, FORCE, jnp.where(blk <= cur, local, -1.0))
    return jnp.where(blk < n_sb, imp, -2.0)


def _cparams(sem):
    return pltpu.CompilerParams(dimension_semantics=sem, vmem_limit_bytes=VMEM_LIMIT)


def _ffn_kernel(x_ref, g_ref, wu_ref, wv_ref, wo_ref, o_ref, h_ref, acc_ref):
    f = pl.program_id(1)

    @pl.when(f == 0)
    def _():
        h_ref[...] = _rms(x_ref[...], g_ref[...]).astype(BF16)
        acc_ref[...] = jnp.zeros_like(acc_ref)

    h = h_ref[...]
    u = _dot(h, wu_ref[...])
    v = _dot(h, wv_ref[...])
    act = (u * _sigmoid(u)) * v
    acc_ref[...] += _dot(act.astype(BF16), wo_ref[...])

    @pl.when(f == pl.num_programs(1) - 1)
    def _():
        o_ref[...] = x_ref[...] + 0.5 * acc_ref[...]


def _ffn(x, norm_ffn, w_in, w_out, l, hf):
    m, d = x.shape
    ff = w_out.shape[2]
    tf = ff // N_FF_TILES
    tm = min(512, m)
    return pl.pallas_call(
        _ffn_kernel,
        grid=(m // tm, N_FF_TILES),
        in_specs=[
            pl.BlockSpec((tm, d), lambda i, f: (i, 0)),
            pl.BlockSpec((None, None, 1, d), lambda i, f: (l, hf, 0, 0)),
            pl.BlockSpec((None, None, d, tf), lambda i, f: (l, hf, 0, f)),
            pl.BlockSpec((None, None, d, tf), lambda i, f: (l, hf, 0, N_FF_TILES + f)),
            pl.BlockSpec((None, None, tf, d), lambda i, f: (l, hf, f, 0)),
        ],
        out_specs=pl.BlockSpec((tm, d), lambda i, f: (i, 0)),
        out_shape=jax.ShapeDtypeStruct((m, d), F32),
        scratch_shapes=[pltpu.VMEM((tm, d), BF16), pltpu.VMEM((tm, d), F32)],
        compiler_params=_cparams(("parallel", "arbitrary")),
        name="ffn_half",
    )(x, norm_ffn, w_in, w_in, w_out)


def _pos_columns(pos):
    lane = lax.broadcasted_iota(jnp.int32, (pos.shape[0], LANES), 1)
    hi = ((pos // L_SEL) * L_SEL).astype(F32)
    lo = (pos % L_SEL).astype(F32)
    return jnp.where(lane < HEAD_DIM + 3, hi, lo) * jnp.where(lane < HEAD_DIM, 0.0, jnp.where(lane < HEAD_DIM + 6, 1.0, 0.0))


def _augmented_keys(kn, pos_cols):
    lane = lax.broadcasted_iota(jnp.int32, pos_cols.shape, 1)
    out = []
    for g in range(kn.shape[1] // HEAD_DIM):
        v = kn[:, (g // 2) * LANES:(g // 2 + 1) * LANES]
        if g % 2:
            v = pltpu.roll(v, HEAD_DIM, 1)
        out.append(jnp.where(lane < HEAD_DIM, v, pos_cols).astype(BF16))
    return out


def _inproj_kernel(x_ref, gm_ref, wa_ref, wb_ref, wq_ref, wkv_ref, wg_ref, nq_ref, nk1_ref, nk2_ref,
                   a_ref, bg_ref, cx_ref, kvc_ref, kvs_ref, kvw_ref, *attn_refs, seq):
    tm = x_ref.shape[0]
    h = _rms(x_ref[...], gm_ref[...]).astype(BF16)
    za = _dot(h, wa_ref[...])
    a_ref[...] = za[:, :D_A] * _sigmoid(za[:, D_A:])
    zb = _dot(h, wb_ref[...])
    bg_ref[...] = zb[:, :D_B]
    cx_ref[...] = zb[:, D_B:2 * D_B] * zb[:, 2 * D_B:]
    qn = _head_rms(_dot(h, wq_ref[...]), nq_ref[...])
    zkv = _dot(h, wkv_ref[...])
    kvc_ref[...] = zkv[:, :2 * KV_W]
    ks = _head_rms(zkv[:, 2 * KV_W:3 * KV_W], nk1_ref[...])
    vs = zkv[:, 3 * KV_W:4 * KV_W]
    kvs_ref[:, :KV_W] = ks
    kvs_ref[:, KV_W:] = vs
    kw = _head_rms(zkv[:, 4 * KV_W:5 * KV_W], nk2_ref[...])
    vw = zkv[:, 5 * KV_W:]
    kvw_ref[:, :KV_W] = kw
    kvw_ref[:, KV_W:] = vw
    gates = _sigmoid(_dot(h, wg_ref[...]))
    if seq is None:
        q_ref, gt_ref = attn_refs
        q_ref[...] = qn.astype(BF16)
        gt_ref[...] = gates
        return
    qt_ref, gtt_ref, ksa_ref, vst_ref, kwa_ref, vwt_ref = attn_refs
    qt_ref[...] = qn.T.astype(BF16)
    gtt_ref[...] = gates.T
    pos = (pl.program_id(0) % (seq // tm)) * tm + lax.broadcasted_iota(jnp.int32, (tm, 1), 0)
    pos_cols = _pos_columns(pos)
    lane = lax.broadcasted_iota(jnp.int32, (tm, LANES), 1)
    onehot = jnp.where(lane == pos // L_SEL, 1.0, 0.0).astype(BF16)
    for g, blk in enumerate(_augmented_keys(ks, pos_cols)):
        ksa_ref[:, 2 * g * LANES:(2 * g + 1) * LANES] = blk
        ksa_ref[:, (2 * g + 1) * LANES:(2 * g + 2) * LANES] = onehot
    for g, blk in enumerate(_augmented_keys(kw, pos_cols)):
        kwa_ref[:, g * LANES:(g + 1) * LANES] = blk
    vst_ref[0] = vs.T.astype(BF16)
    vwt_ref[...] = vw.T.astype(BF16)


def _inproj(x, p, l, seq=None):
    m, d = x.shape
    tm = min(256, m)

    def wspec(w):
        return pl.BlockSpec((None,) + w.shape[1:], lambda i: (l, 0, 0))

    def rows(n, dt):
        return pl.BlockSpec((tm, n), lambda i: (i, 0)), jax.ShapeDtypeStruct((m, n), dt)

    def cols(n, dt):
        return pl.BlockSpec((n, tm), lambda i: (0, i)), jax.ShapeDtypeStruct((n, m), dt)

    outs = [rows(D_A, F32), rows(D_B, F32), rows(D_B, F32),
            rows(2 * KV_W, F32), rows(2 * KV_W, F32), rows(2 * KV_W, F32)]
    if seq is None:
        outs += [rows(N_HEADS * HEAD_DIM, BF16), rows(LANES, F32)]
    else:
        assert seq % tm == 0 and seq // L_SEL <= LANES
        outs += [cols(N_HEADS * HEAD_DIM, BF16), cols(LANES, F32), rows(2 * N_KV * LANES, BF16),
                 (pl.BlockSpec((1, KV_W, tm), lambda i: (i, 0, 0)), jax.ShapeDtypeStruct((m // tm, KV_W, tm), BF16)),
                 rows(N_KV * LANES, BF16), cols(KV_W, BF16)]
    ws = [p["norm_mix"], p["w_a"], p["w_b"], p["w_q"], p["w_kv"], p["w_g"], p["nq"], p["nk1"], p["nk2"]]
    return pl.pallas_call(
        functools.partial(_inproj_kernel, seq=seq),
        grid=(m // tm,),
        in_specs=[pl.BlockSpec((tm, d), lambda i: (i, 0))] + [wspec(w) for w in ws],
        out_specs=[o[0] for o in outs],
        out_shape=[o[1] for o in outs],
        compiler_params=_cparams(("parallel",)),
        name="mixer_inproj",
    )(x, *ws)


CONV_HALO = 32
CONV_CHUNK = 64


def _conv_p_kernel(a_ref, ah_ref, cx_ref, ch_ref, bg_ref, wa_ref, ba_ref, ga_ref, wb_ref,
                   ca_ref, yb_ref, xa_ref, xb_ref):
    first = pl.program_id(1) == 0
    tc = a_ref.shape[0]
    xa_ref[0:CONV_HALO, :] = jnp.where(first, 0.0, ah_ref[...])
    xa_ref[CONV_HALO:, :] = a_ref[...]
    xb_ref[0:CONV_HALO, :] = jnp.where(first, 0.0, ch_ref[...])
    xb_ref[CONV_HALO:, :] = cx_ref[...]
    for c in range(tc // CONV_CHUNK):
        r0 = c * CONV_CHUNK
        acc = None
        for k in range(CONV_A):
            o = r0 + CONV_HALO - (CONV_A - 1) + k
            term = wa_ref[k:k + 1, :] * xa_ref[o:o + CONV_CHUNK, :]
            acc = term if acc is None else acc + term
        y = _rms(acc + ba_ref[...], ga_ref[...])
        ca_ref[r0:r0 + CONV_CHUNK, :] = (y * _sigmoid(y)).astype(BF16)
        acc = None
        for k in range(CONV_B):
            o = r0 + CONV_HALO - (CONV_B - 1) + k
            term = wb_ref[k:k + 1, :] * xb_ref[o:o + CONV_CHUNK, :]
            acc = term if acc is None else acc + term
        yb_ref[r0:r0 + CONV_CHUNK, :] = (bg_ref[r0:r0 + CONV_CHUNK, :] * acc).astype(BF16)


def _conv_prompt(a, cx, bg, p, l, batch, seq):
    tc = 256
    nt = seq // tc
    hb = tc // CONV_HALO

    def cur(b, i):
        return (b * nt + i, 0)

    def halo(b, i):
        return (jnp.maximum((b * nt + i) * hb - 1, 0), 0)

    def wspec(w):
        return pl.BlockSpec((None,) + w.shape[1:], lambda b, i: (l, 0, 0))

    ws = [p["w_conv_a"], p["b_conv_a"], p["norm_conv_a"], p["w_conv_b"]]
    m = batch * seq
    return pl.pallas_call(
        _conv_p_kernel,
        grid=(batch, nt),
        in_specs=[pl.BlockSpec((tc, D_A), cur), pl.BlockSpec((CONV_HALO, D_A), halo),
                  pl.BlockSpec((tc, D_B), cur), pl.BlockSpec((CONV_HALO, D_B), halo),
                  pl.BlockSpec((tc, D_B), cur)] + [wspec(w) for w in ws],
        out_specs=[pl.BlockSpec((tc, D_A), cur), pl.BlockSpec((tc, D_B), cur)],
        out_shape=[jax.ShapeDtypeStruct((m, D_A), BF16), jax.ShapeDtypeStruct((m, D_B), BF16)],
        scratch_shapes=[pltpu.VMEM((tc + CONV_HALO, D_A), F32), pltpu.VMEM((tc + CONV_HALO, D_B), F32)],
        compiler_params=_cparams(("parallel", "arbitrary")),
        name="conv_prompt",
    )(a, a, cx, cx, bg, *ws)


def _conv_s_kernel(bufa_ref, a_ref, bufb_ref, cx_ref, bg_ref, wa_ref, ba_ref, ga_ref, wb_ref,
                   ca_ref, yb_ref, na_ref, nb_ref, *, t_new):
    def col(j, w):
        return slice(j * w, (j + 1) * w)

    def up_a(j):
        return bufa_ref[:, col(j, D_A)] if j < CONV_A - 1 else a_ref[:, col(j - (CONV_A - 1), D_A)]

    def up_b(j):
        return bufb_ref[:, col(j, D_B)] if j < CONV_B - 1 else cx_ref[:, col(j - (CONV_B - 1), D_B)]

    for t in range(t_new):
        acc = None
        for k in range(CONV_A):
            term = wa_ref[k:k + 1, :] * up_a(t + k)
            acc = term if acc is None else acc + term
        y = _rms(acc + ba_ref[...], ga_ref[...])
        ca_ref[:, col(t, D_A)] = (y * _sigmoid(y)).astype(BF16)
        acc = None
        for k in range(CONV_B):
            term = wb_ref[k:k + 1, :] * up_b(t + k)
            acc = term if acc is None else acc + term
        yb_ref[:, col(t, D_B)] = (bg_ref[:, col(t, D_B)] * acc).astype(BF16)
    for j in range(CONV_A - 1):
        na_ref[:, col(j, D_A)] = up_a(j + t_new)
    for j in range(CONV_B - 1):
        nb_ref[:, col(j, D_B)] = up_b(j + t_new)


def _conv_sample(bufa, a, bufb, cx, bg, p, l, t_new):
    ns = a.shape[0]
    tn = min(64, ns)

    def rows(n):
        return pl.BlockSpec((tn, n), lambda i: (i, 0))

    def wspec(w):
        return pl.BlockSpec((None,) + w.shape[1:], lambda i: (l, 0, 0))

    ws = [p["w_conv_a"], p["b_conv_a"], p["norm_conv_a"], p["w_conv_b"]]
    ins = [bufa, a, bufb, cx, bg]
    return pl.pallas_call(
        functools.partial(_conv_s_kernel, t_new=t_new),
        grid=(ns // tn,),
        in_specs=[rows(x.shape[1]) for x in ins] + [wspec(w) for w in ws],
        out_specs=[rows(a.shape[1]), rows(cx.shape[1]), rows(bufa.shape[1]), rows(bufb.shape[1])],
        out_shape=[jax.ShapeDtypeStruct(a.shape, BF16), jax.ShapeDtypeStruct(cx.shape, BF16),
                   jax.ShapeDtypeStruct(bufa.shape, F32), jax.ShapeDtypeStruct(bufb.shape, F32)],
        compiler_params=_cparams(("parallel",)),
        name="conv_sample",
    )(*ins, *ws)


N_COL_BLOCKS = 2 * KV_W // LANES


def _compress_cols(x_ref, r_ref, pe_ref, nk_ref, lhs_ref, z_ref, n_chunks, is_k):
    z = None
    for m in range(D_CMP):
        lhs_ref[0:n_chunks, :] = x_ref[pl.ds(m, n_chunks, stride=D_CMP), :].astype(BF16)
        lhs_ref[n_chunks:, :] = pe_ref[m]
        zz = _dot(lhs_ref[...], r_ref[m])
        z = zz if z is None else z + zz
    z_ref[...] = z
    pe_bias = z_ref[n_chunks:n_chunks + 1, 0:LANES] + z_ref[n_chunks + 1:n_chunks + 2, LANES:]
    comp = z_ref[0:n_chunks, 0:LANES] + z_ref[1:n_chunks + 1, LANES:] + pe_bias
    return _head_rms(comp, nk_ref[...]) if is_k else comp


def _compress_p_kernel(x_ref, r_ref, pe_ref, nk_ref, kca_ref, vct_ref, lhs_ref, z_ref):
    n_chunks = kca_ref.shape[0]
    half = N_COL_BLOCKS // 2

    @pl.when(pl.program_id(1) < half)
    def _():
        kn = _compress_cols(x_ref, r_ref, pe_ref, nk_ref, lhs_ref, z_ref, n_chunks, True)
        c_end = lax.broadcasted_iota(jnp.int32, (n_chunks, 1), 0) * D_CMP + (L_CMP - 1)
        for g, blk in enumerate(_augmented_keys(kn, _pos_columns(c_end))):
            kca_ref[:, g * LANES:(g + 1) * LANES] = blk

    @pl.when(pl.program_id(1) >= half)
    def _():
        comp = _compress_cols(x_ref, r_ref, pe_ref, nk_ref, lhs_ref, z_ref, n_chunks, False)
        vct_ref[...] = comp.T.astype(BF16)


def _compress_prompt(kvc, p, l, batch, seq):
    nck = seq // D_CMP
    half = N_COL_BLOCKS // 2
    return pl.pallas_call(
        _compress_p_kernel,
        grid=(batch, N_COL_BLOCKS),
        in_specs=[pl.BlockSpec((seq, LANES), lambda b, c: (b, c)),
                  pl.BlockSpec((None, None, D_CMP, LANES, 2 * LANES), lambda b, c: (l, c // half, 0, 0, 0)),
                  pl.BlockSpec((None, None, D_CMP, 16, LANES), lambda b, c: (l, c // half, 0, 0, 0)),
                  pl.BlockSpec((None, 1, LANES), lambda b, c: (l, 0, 0))],
        out_specs=[pl.BlockSpec((None, nck, 2 * LANES), lambda b, c: (b, 0, jnp.minimum(c, half - 1))),
                   pl.BlockSpec((None, LANES, nck), lambda b, c: (b, jnp.maximum(c - half, 0), 0))],
        out_shape=[jax.ShapeDtypeStruct((batch, nck, N_KV * LANES), BF16),
                   jax.ShapeDtypeStruct((batch, KV_W, nck), BF16)],
        scratch_shapes=[pltpu.VMEM((nck + 16, LANES), BF16), pltpu.VMEM((nck + 16, 2 * LANES), F32)],
        compiler_params=_cparams(("arbitrary", "arbitrary")),
        name="compress_prompt",
    )(kvc, p["r_phi"], p["pe16"], p["nk0"])


def _compress_s_kernel(pt_ref, cache_ref, r_ref, pe_ref, nk_ref, o_ref, pg_ref, x_ref, lhs_ref, z_ref, sem,
                       *, layer, n_pages, n_samples):
    j = pl.program_id(0)

    def page_copy(i):
        page = pt_ref[j * n_samples * n_pages + i]
        return pltpu.make_async_copy(cache_ref.at[layer, page], pg_ref.at[i], sem.at[0])

    for i in range(n_samples * n_pages):
        page_copy(i).start()
    for i in range(n_samples * n_pages):
        page_copy(i).wait()
    for i in range(n_samples * n_pages):
        for c in range(N_COL_BLOCKS):
            x_ref[c, i * PAGE_SIZE:(i + 1) * PAGE_SIZE, :] = pg_ref[i, c * LANES:(c + 1) * LANES, :].T
    n_chunks = o_ref.shape[0]
    half = N_COL_BLOCKS // 2
    for c in range(N_COL_BLOCKS):
        o_ref[:, c * LANES:(c + 1) * LANES] = _compress_cols(
            x_ref.at[c], r_ref.at[c // half], pe_ref.at[c // half], nk_ref, lhs_ref, z_ref,
            n_chunks, c < half).astype(BF16)


def _compress_sample(cache, page_table, p, l):
    ns, n_pages = page_table.shape
    spb = min(SAMPLES_PER_COMPRESS_STEP, ns)
    rows = spb * n_pages * PAGE_SIZE
    nck = rows // D_CMP
    grid_spec = pltpu.PrefetchScalarGridSpec(
        num_scalar_prefetch=1,
        grid=(ns // spb,),
        in_specs=[pl.BlockSpec(memory_space=pl.ANY),
                  pl.BlockSpec((None, 2, D_CMP, LANES, 2 * LANES), lambda j, pt: (l, 0, 0, 0, 0)),
                  pl.BlockSpec((None, 2, D_CMP, 16, LANES), lambda j, pt: (l, 0, 0, 0, 0)),
                  pl.BlockSpec((None, 1, LANES), lambda j, pt: (l, 0, 0))],
        out_specs=pl.BlockSpec((nck, 2 * KV_W), lambda j, pt: (j, 0)),
        scratch_shapes=[pltpu.VMEM((spb * n_pages, 2 * KV_W, PAGE_SIZE), F32),
                        pltpu.VMEM((N_COL_BLOCKS, rows, LANES), F32), pltpu.VMEM((nck + 16, LANES), BF16),
                        pltpu.VMEM((nck + 16, 2 * LANES), F32), pltpu.SemaphoreType.DMA((1,))],
    )
    return pl.pallas_call(
        functools.partial(_compress_s_kernel, layer=l, n_pages=n_pages, n_samples=spb),
        grid_spec=grid_spec,
        out_shape=jax.ShapeDtypeStruct((ns * n_pages * PAGE_SIZE // D_CMP, 2 * KV_W), BF16),
        compiler_params=_cparams(("arbitrary",)),
        name="compress_sample",
    )(page_table.reshape(-1), cache, p["r_phi"], p["pe16"], p["nk0"])


N_WIN_BLOCKS = WINDOW // Q_BLOCK + 1
Q_AUG_ROWS = 2 * LANES
SLOPE_ROWS = 16


def _attn_p_kernel(qt_ref, gtt_ref, kca_ref, vct_ref, ovt_ref, srow_ref, ksa_ref, vst_ref, *rest, n_sb):
    kw_refs = rest[:N_WIN_BLOCKS]
    vw_refs = rest[N_WIN_BLOCKS:2 * N_WIN_BLOCKS]
    o_ref, qa_ref, acc_ref, m_ref, l_ref, ocw_ref, flag_ref = rest[2 * N_WIN_BLOCKS:]
    tq = qt_ref.shape[1]
    cols = GROUP * tq
    ncp = kca_ref.shape[0]
    n_tiles = ksa_ref.shape[0] // SEL_TILE
    qi = pl.program_id(1)
    t0 = qi * tq
    tok = t0 + lax.broadcasted_iota(jnp.int32, (1, tq), 1)
    tok4 = jnp.concatenate([tok] * GROUP, axis=1)

    def gate_row(g, br):
        rows = [N_BRANCH * (g * GROUP + r) + br for r in range(GROUP)]
        return jnp.concatenate([gtt_ref[c:c + 1, :] for c in rows], axis=1)

    @pl.when((pl.program_id(0) == 0) & (qi == 0))
    def _():
        for g in range(N_KV):
            qa_ref[g, HEAD_DIM:HEAD_DIM + SLOPE_ROWS, :] = srow_ref[g]
            qa_ref[g, HEAD_DIM + SLOPE_ROWS:LANES, :] = jnp.zeros((LANES - HEAD_DIM - SLOPE_ROWS, cols), BF16)

    for g in range(N_KV):
        for r in range(GROUP):
            h = g * GROUP + r
            qa_ref[g, 0:HEAD_DIM, r * tq:(r + 1) * tq] = qt_ref[h * HEAD_DIM:(h + 1) * HEAD_DIM, :]

    cpos = lax.broadcasted_iota(jnp.int32, (ncp, 1), 0) * D_CMP + (L_CMP - 1)
    cmask = cpos <= tok4
    imps = []
    for g in range(N_KV):
        s = _dot(kca_ref[:, g * LANES:(g + 1) * LANES], qa_ref[g, 0:LANES, :])
        s = jnp.where(cmask, s, NEG)
        p = jnp.where(cmask, jnp.exp(s - jnp.max(s, axis=0, keepdims=True)), 0.0)
        den = jnp.sum(p, axis=0, keepdims=True)
        pc = p / jnp.where(den > 0, den, 1.0)
        psum = pc[:, 0:tq]
        for r in range(1, GROUP):
            psum = psum + pc[:, r * tq:(r + 1) * tq]
        hi = psum.astype(BF16)
        lo = (psum - hi.astype(F32)).astype(BF16)
        imps.append(_dot(ovt_ref[...], hi) + _dot(ovt_ref[...], lo))
        ocw_ref[g] = gate_row(g, 0) * _dot(vct_ref[g * HEAD_DIM:(g + 1) * HEAD_DIM, :], pc.astype(BF16))

    blk = lax.broadcasted_iota(jnp.int32, (LANES, 1), 0)
    cur = tok4 // L_SEL
    imp = _forced_importance(jnp.concatenate(imps, axis=1), blk, cur, n_sb)
    sel = jnp.where(blk <= cur, _topk_mask(imp, min(N_SEL, n_sb), 0), 0.0)
    selb = jnp.where(sel > 0.5, 0.0, NEG).astype(BF16)
    blocks_per_tile = SEL_TILE // L_SEL
    for g in range(N_KV):
        qa_ref[g, LANES:, :] = jnp.concatenate([selb[:, g * tq:(g + 1) * tq]] * GROUP, axis=1)
        for kt in range(n_tiles):
            used = jnp.max(sel[kt * blocks_per_tile:(kt + 1) * blocks_per_tile, g * tq:(g + 1) * tq])
            flag_ref[g * n_tiles + kt] = (used > 0.5).astype(jnp.int32)

    wpos = t0 - WINDOW + lax.broadcasted_iota(jnp.int32, (N_WIN_BLOCKS * tq, 1), 0)
    wdist = tok4 - wpos
    wbias = jnp.where(wpos >= 0, jnp.where(wdist >= 0, jnp.where(wdist <= WINDOW, 0.0, NEG), NEG), NEG)
    for g in range(N_KV):
        kwin = jnp.concatenate([kr[:, g * LANES:(g + 1) * LANES] for kr in kw_refs], axis=0)
        vwin = jnp.concatenate([vr[g * HEAD_DIM:(g + 1) * HEAD_DIM, :] for vr in vw_refs], axis=1)
        s = _dot(kwin, qa_ref[g, 0:LANES, :]) + wbias
        p = jnp.exp(s - jnp.max(s, axis=0, keepdims=True))
        den = jnp.sum(p, axis=0, keepdims=True)
        ocw_ref[g] += gate_row(g, 2) * (_dot(vwin, p.astype(BF16)) / den)

    m_ref[...] = jnp.full(m_ref.shape, -1e29, F32)
    l_ref[...] = jnp.zeros(l_ref.shape, F32)
    acc_ref[...] = jnp.zeros(acc_ref.shape, F32)
    half = SEL_TILE // 2

    def tile_update(kt, g, extra):
        k0 = pl.multiple_of(kt * SEL_TILE, SEL_TILE)
        s = _dot(ksa_ref[pl.ds(k0, SEL_TILE), 2 * g * LANES:(2 * g + 2) * LANES], qa_ref[g])
        if extra is not None:
            s = s + extra
        m_prev = m_ref[g]
        m_new = jnp.maximum(m_prev, jnp.max(s, axis=0, keepdims=True))
        p = jnp.exp(s - m_new)
        alpha = jnp.exp(m_prev - m_new)
        l_ref[g] = alpha * l_ref[g] + jnp.sum(p, axis=0, keepdims=True)
        pb = p.astype(BF16)
        pv = _dot(vst_ref[2 * kt, g * HEAD_DIM:(g + 1) * HEAD_DIM, :], pb[0:half])
        pv += _dot(vst_ref[2 * kt + 1, g * HEAD_DIM:(g + 1) * HEAD_DIM, :], pb[half:])
        acc_ref[g] = acc_ref[g] * alpha + pv
        m_ref[g] = m_new

    n_full = t0 // SEL_TILE

    def sweep(kt, carry):
        for g in range(N_KV):
            @pl.when(flag_ref[g * n_tiles + kt] > 0)
            def _():
                tile_update(kt, g, None)
        return carry

    lax.fori_loop(0, n_full, sweep, 0)
    kpos = n_full * SEL_TILE + lax.broadcasted_iota(jnp.int32, (SEL_TILE, 1), 0)
    causal = jnp.where(kpos <= tok4, 0.0, NEG)
    for g in range(N_KV):
        tile_update(n_full, g, causal)

    for g in range(N_KV):
        den = l_ref[g]
        o_t = ocw_ref[g] + gate_row(g, 1) * (acc_ref[g] / jnp.where(den > 0, den, 1.0))
        for pair in range(GROUP // 2):
            two = jnp.concatenate([o_t[:, (2 * pair) * tq:(2 * pair + 1) * tq],
                                   o_t[:, (2 * pair + 1) * tq:(2 * pair + 2) * tq]], axis=0)
            c0 = (g * GROUP + 2 * pair) * HEAD_DIM
            o_ref[:, c0:c0 + 2 * HEAD_DIM] = two.T.astype(o_ref.dtype)


def _attn_prompt(qt, gtt, kca, vct, ovt, srow, ksa, vst, kwa, vwt, batch, seq):
    tq = Q_BLOCK
    nqb = seq // tq
    nck = seq // D_CMP
    n_sb = seq // L_SEL
    chunks = vst.shape[0] // batch
    assert vst.shape[2] * 2 == SEL_TILE and 2 * HEAD_DIM == LANES

    def qcol(b, i):
        return (0, b * nqb + i)

    def win_rows(j):
        return lambda b, i: (b * nqb + jnp.maximum(i - (N_WIN_BLOCKS - 1) + j, 0), 0)

    def win_cols(j):
        return lambda b, i: (0, b * nqb + jnp.maximum(i - (N_WIN_BLOCKS - 1) + j, 0))

    in_specs = [pl.BlockSpec((N_HEADS * HEAD_DIM, tq), qcol),
                pl.BlockSpec((LANES, tq), qcol),
                pl.BlockSpec((None, nck, N_KV * LANES), lambda b, i: (b, 0, 0)),
                pl.BlockSpec((None, KV_W, nck), lambda b, i: (b, 0, 0)),
                pl.BlockSpec(ovt.shape, lambda b, i: (0, 0)),
                pl.BlockSpec(srow.shape, lambda b, i: (0, 0, 0)),
                pl.BlockSpec((seq, 2 * N_KV * LANES), lambda b, i: (b, 0), pipeline_mode=pl.Buffered(1)),
                pl.BlockSpec((chunks, KV_W, vst.shape[2]), lambda b, i: (b, 0, 0), pipeline_mode=pl.Buffered(1))]
    in_specs += [pl.BlockSpec((tq, N_KV * LANES), win_rows(j)) for j in range(N_WIN_BLOCKS)]
    in_specs += [pl.BlockSpec((KV_W, tq), win_cols(j)) for j in range(N_WIN_BLOCKS)]
    cols = GROUP * tq
    return pl.pallas_call(
        functools.partial(_attn_p_kernel, n_sb=n_sb),
        grid=(batch, nqb),
        in_specs=in_specs,
        out_specs=pl.BlockSpec((tq, N_HEADS * HEAD_DIM), lambda b, i: (b * nqb + i, 0)),
        out_shape=jax.ShapeDtypeStruct((batch * seq, N_HEADS * HEAD_DIM), BF16),
        scratch_shapes=[pltpu.VMEM((N_KV, Q_AUG_ROWS, cols), BF16), pltpu.VMEM((N_KV, HEAD_DIM, cols), F32),
                        pltpu.VMEM((N_KV, 1, cols), F32), pltpu.VMEM((N_KV, 1, cols), F32),
                        pltpu.VMEM((N_KV, HEAD_DIM, cols), F32),
                        pltpu.SMEM((N_KV * (seq // SEL_TILE),), jnp.int32)],
        compiler_params=_cparams(("arbitrary", "arbitrary")),
        name="nsa_prompt",
    )(qt, gtt, kca, vct, ovt, srow, ksa, vst, *([kwa] * N_WIN_BLOCKS), *([vwt] * N_WIN_BLOCKS))


SAMPLES_PER_ATTN_STEP = 2


def _attn_s_kernel(pt_ref, q_ref, gt_ref, kvc_ref, ov_ref, ex_ref, kvs_ref, kvw_ref, win_ref, cache_ref,
                   o_ref, nwin_ref, ksel_ref, imp_ref, sem, *, layer, n_pages, t_new, n_sb):
    spb = q_ref.shape[0]
    n0 = pl.program_id(0) * spb
    past = n_pages * PAGE_SIZE
    w_buf = win_ref.shape[2]
    rows = GROUP * t_new
    gt_rows = N_KV * t_new
    ncp = kvc_ref.shape[0] // spb

    def page_copy(s, pg):
        dst = ksel_ref.at[s, :, pl.ds(pg * PAGE_SIZE, PAGE_SIZE)]
        return pltpu.make_async_copy(cache_ref.at[layer, pt_ref[(n0 + s) * n_pages + pg]], dst, sem.at[0])

    for s in range(spb):
        for pg in range(n_pages):
            page_copy(s, pg).start()

    kwin = {}
    for s in range(spb):
        kwin[s] = jnp.concatenate([win_ref[s], kvw_ref[s]], axis=1)
        nwin_ref[s] = pltpu.roll(kwin[s], kwin[s].shape[1] - t_new, 1)[:, 0:w_buf]

    ridx = lax.broadcasted_iota(jnp.int32, (rows, 1), 0)
    tpos = past + ridx % t_new

    def head_slopes(g):
        out = jnp.zeros((rows, 1), F32)
        for r in range(GROUP):
            out = jnp.where(ridx // t_new == r, SLOPES[g * GROUP + r], out)
        return out

    def attend(scores, kpos, mask, sl):
        s = scores - sl * (tpos - kpos).astype(F32)
        s = jnp.where(mask, s, NEG)
        p = jnp.where(mask, jnp.exp(s - jnp.max(s, axis=1, keepdims=True)), 0.0)
        den = jnp.sum(p, axis=1, keepdims=True)
        return p / jnp.where(den > 0, den, 1.0)

    def group_rows(x, g):
        return x[g * HEAD_DIM:(g + 1) * HEAD_DIM, :].astype(BF16), \
            x[KV_W + g * HEAD_DIM:KV_W + (g + 1) * HEAD_DIM, :].astype(BF16)

    cpos = lax.broadcasted_iota(jnp.int32, (1, ncp), 1) * D_CMP + (L_CMP - 1)
    o_cmp = {}
    for s in range(spb):
        for g in range(N_KV):
            kcg = kvc_ref[s * ncp:(s + 1) * ncp, g * HEAD_DIM:(g + 1) * HEAD_DIM]
            vcg = kvc_ref[s * ncp:(s + 1) * ncp, KV_W + g * HEAD_DIM:KV_W + (g + 1) * HEAD_DIM]
            pc = attend(_dot_nt(q_ref[s, g], kcg), cpos, cpos <= tpos, head_slopes(g))
            o_cmp[s, g] = _dot(pc.astype(BF16), vcg)
            imp_rows = _dot_split(pc, ov_ref[...])
            imp = imp_rows[0:t_new]
            for r in range(1, GROUP):
                imp = imp + imp_rows[r * t_new:(r + 1) * t_new]
            imp_ref[s * gt_rows + g * t_new:s * gt_rows + (g + 1) * t_new, :] = imp

    tsel = past + lax.broadcasted_iota(jnp.int32, (spb * gt_rows, 1), 0) % t_new
    blk = lax.broadcasted_iota(jnp.int32, (1, LANES), 1)
    cur = tsel // L_SEL
    imp = _forced_importance(imp_ref[...], blk, cur, n_sb)
    sel = _topk_mask(imp, min(N_SEL, n_sb), 1)
    selb = jnp.where(blk <= cur, jnp.where(sel > 0.5, 0.0, NEG), NEG).astype(BF16)
    key_bias = _dot(selb, ex_ref[...]).astype(BF16)

    n_w = w_buf + kvw_ref.shape[2]
    wpos = past - w_buf + lax.broadcasted_iota(jnp.int32, (1, n_w), 1)
    wdist = tpos - wpos
    wmask = (wdist >= 0) & (wdist <= WINDOW)
    o_win = {}
    for s in range(spb):
        for g in range(N_KV):
            kt, vt = group_rows(kwin[s], g)
            pw = attend(_dot(q_ref[s, g], kt), wpos, wmask, head_slopes(g))
            o_win[s, g] = _dot_nt(pw.astype(BF16), vt)

    for s in range(spb):
        for pg in range(n_pages):
            page_copy(s, pg).wait()
    n_k = ksel_ref.shape[2]
    kpos = lax.broadcasted_iota(jnp.int32, (1, n_k), 1)
    col = lax.broadcasted_iota(jnp.int32, (rows, spb * gt_rows), 1)
    for s in range(spb):
        ksel_ref[s, :, past:] = kvs_ref[s]
        for g in range(N_KV):
            spread = jnp.where(col == s * gt_rows + g * t_new + ridx % t_new, 1.0, 0.0).astype(BF16)
            bias = _dot(spread, key_bias)
            mask = (bias > 0.5 * NEG) & (kpos <= tpos)
            kt, vt = group_rows(ksel_ref.at[s], g)
            ps = attend(_dot(q_ref[s, g], kt), kpos, mask, head_slopes(g))
            o_sel = _dot_nt(ps.astype(BF16), vt)
            gates = gt_ref[s, g]
            o_ref[s, g] = gates[:, 0:1] * o_cmp[s, g] + gates[:, 1:2] * o_sel + gates[:, 2:3] * o_win[s, g]


def _attn_sample(q, gates, kvc, ov, expand, kvs_new, kvw_new, win, cache, page_table, l, t_new):
    ns, n_pages = page_table.shape
    spb = min(SAMPLES_PER_ATTN_STEP, ns)
    past = n_pages * PAGE_SIZE
    w_buf = win.shape[3]
    n_sb = -(-(past + t_new) // L_SEL)
    n_k = past + kvs_new.shape[2]
    ncp = past // D_CMP
    rows = GROUP * t_new

    def per_sample(*blk):
        return pl.BlockSpec((spb,) + blk, lambda n, pt: (n,) + (0,) * len(blk))

    grid_spec = pltpu.PrefetchScalarGridSpec(
        num_scalar_prefetch=1,
        grid=(ns // spb,),
        in_specs=[per_sample(N_KV, rows, HEAD_DIM), per_sample(N_KV, rows, N_BRANCH),
                  pl.BlockSpec((spb * ncp, 2 * KV_W), lambda n, pt: (n, 0)),
                  pl.BlockSpec(ov.shape, lambda n, pt: (0, 0)),
                  pl.BlockSpec(expand.shape, lambda n, pt: (0, 0)),
                  per_sample(*kvs_new.shape[1:]), per_sample(*kvw_new.shape[1:]),
                  pl.BlockSpec((None, spb, 2 * KV_W, w_buf), lambda n, pt: (l, n, 0, 0)),
                  pl.BlockSpec(memory_space=pl.ANY)],
        out_specs=[per_sample(N_KV, rows, HEAD_DIM), per_sample(2 * KV_W, w_buf)],
        scratch_shapes=[pltpu.VMEM((spb, 2 * KV_W, n_k), F32),
                        pltpu.VMEM((spb * N_KV * t_new, LANES), F32), pltpu.SemaphoreType.DMA((1,))],
    )
    return pl.pallas_call(
        functools.partial(_attn_s_kernel, layer=l, n_pages=n_pages, t_new=t_new, n_sb=n_sb),
        grid_spec=grid_spec,
        out_shape=[jax.ShapeDtypeStruct((ns, N_KV, rows, HEAD_DIM), F32),
                   jax.ShapeDtypeStruct((ns, 2 * KV_W, w_buf), F32)],
        compiler_params=_cparams(("arbitrary",)),
        name="nsa_sample",
    )(page_table.reshape(-1), q, gates, kvc, ov, expand, kvs_new, kvw_new, win, cache)


def _merge_kernel(x_ref, gm_ref, ca_ref, yb_ref, oc_ref, wm_ref, woa_ref, wob_ref, woc_ref, wo_ref, o_ref):
    x = x_ref[...]
    d = x.shape[1]
    h = _rms(x, gm_ref[...]).astype(BF16)
    mix = _sigmoid(_dot(h, wm_ref[:, 0:d])) * _dot(ca_ref[...], woa_ref[...])
    mix += _sigmoid(_dot(h, wm_ref[:, d:2 * d])) * _dot(yb_ref[...], wob_ref[...])
    mix += _sigmoid(_dot(h, wm_ref[:, 2 * d:])) * _dot(oc_ref[...], woc_ref[...])
    o_ref[...] = x + _dot(mix.astype(BF16), wo_ref[...])


def _merge(x, ca, yb, oc, p, l):
    m, d = x.shape
    tm = min(256, m)

    def wspec(w):
        return pl.BlockSpec((None,) + w.shape[1:], lambda i: (l, 0, 0))

    def rows(a):
        return pl.BlockSpec((tm, a.shape[1]), lambda i: (i, 0))

    ws = [p["w_m"], p["w_out_a"], p["w_out_b"], p["w_out_c"], p["w_o"]]
    return pl.pallas_call(
        _merge_kernel,
        grid=(m // tm,),
        in_specs=[rows(x), wspec(p["norm_mix"]), rows(ca), rows(yb), rows(oc)] + [wspec(w) for w in ws],
        out_specs=rows(x),
        out_shape=jax.ShapeDtypeStruct((m, d), F32),
        compiler_params=_cparams(("parallel",)),
        name="mixer_merge",
    )(x, p["norm_mix"], ca, yb, oc, *ws)


def _prepare_params(norm_ffn, w_ffn_in, w_ffn_out, norm_mix, w_in, w_conv_a, b_conv_a, norm_conv_a,
                    w_out_a, w_conv_b, w_out_b, norm_q, norm_k, pe_cmp, w_phi, w_out_c, w_o):
    depth, d = norm_mix.shape
    sizes = (2 * D_A, 3 * D_B, N_HEADS * HEAD_DIM, 6 * KV_W, N_BRANCH * N_HEADS, N_BRANCH * d)
    offs = [0] + [int(v) for v in np.cumsum(sizes)]
    seg = [w_in[:, :, offs[i]:offs[i + 1]].astype(BF16) for i in range(6)]
    w_g = jnp.pad(seg[4], ((0, 0), (0, 0), (0, LANES - sizes[4])))
    gpb = LANES // HEAD_DIM
    w = w_phi.reshape(depth, 2, L_CMP // D_CMP, D_CMP, HEAD_DIM, HEAD_DIM)
    r_phi = jnp.einsum("leimdf,gh->lemgdihf", w, jnp.eye(gpb, dtype=w.dtype))
    r_phi = r_phi.reshape(depth, 2, D_CMP, LANES, 2 * LANES).astype(BF16)
    pe = pe_cmp.reshape(depth, L_CMP // D_CMP, D_CMP, 2, HEAD_DIM).transpose(0, 3, 2, 1, 4)
    pe = jnp.tile(pe, (1, 1, 1, 1, gpb))
    pe16 = jnp.pad(pe, ((0, 0), (0, 0), (0, 0), (0, 16 - L_CMP // D_CMP), (0, 0))).astype(BF16)
    return {
        "norm_ffn": norm_ffn.reshape(depth, 2, 1, d),
        "w_ffn_in": w_ffn_in.astype(BF16), "w_ffn_out": w_ffn_out.astype(BF16),
        "norm_mix": norm_mix.reshape(depth, 1, d),
        "w_a": seg[0], "w_b": seg[1], "w_q": seg[2], "w_kv": seg[3], "w_g": w_g, "w_m": seg[5],
        "nq": jnp.tile(norm_q * HEAD_DIM ** -0.5, (1, N_HEADS)).reshape(depth, 1, N_HEADS * HEAD_DIM),
        "nk0": jnp.tile(norm_k[:, 0], (1, LANES // HEAD_DIM)).reshape(depth, 1, LANES),
        "nk1": jnp.tile(norm_k[:, 1], (1, N_KV)).reshape(depth, 1, KV_W),
        "nk2": jnp.tile(norm_k[:, 2], (1, N_KV)).reshape(depth, 1, KV_W),
        "w_conv_a": w_conv_a, "b_conv_a": b_conv_a.reshape(depth, 1, D_A),
        "norm_conv_a": norm_conv_a.reshape(depth, 1, D_A), "w_conv_b": w_conv_b,
        "w_out_a": w_out_a.astype(BF16), "w_out_b": w_out_b.astype(BF16),
        "w_out_c": w_out_c.astype(BF16), "w_o": w_o.astype(BF16),
        "r_phi": r_phi, "pe16": pe16,
    }


def _overlap_matrix(n_chunk_rows, n_sb):
    c = np.arange(n_chunk_rows)[:, None] * D_CMP
    j = np.arange(LANES)[None, :] * L_SEL
    ov = (c < j + L_SEL) & (c + L_CMP > j) & (np.arange(LANES)[None, :] < n_sb)
    ov &= (np.arange(n_chunk_rows)[:, None] < n_chunk_rows - 1)
    return jnp.asarray(ov.astype(np.float32), dtype=BF16)


def _block_expand_matrix(n_keys):
    e = np.arange(LANES)[:, None] == (np.arange(n_keys)[None, :] // L_SEL)
    return jnp.asarray(e.astype(np.float32), dtype=BF16)


def _slope_rows():
    rows = np.zeros((N_KV, SLOPE_ROWS, GROUP * Q_BLOCK), np.float32)
    for h, s in enumerate(SLOPES):
        rest = np.float32(s)
        for i in range(3):
            piece = np.float32(rest.astype(jnp.bfloat16))
            rest = np.float32(rest - piece)
            rows[h // GROUP, [i, i + 3], (h % GROUP) * Q_BLOCK:(h % GROUP + 1) * Q_BLOCK] = piece
        assert rest == 0.0
    return jnp.asarray(rows, dtype=BF16)


def kernel(x_prompt, x_sample, cache_cmp_kv, cache_sel_kv, state_win_kv, state_conv_a, state_conv_b,
           page_table, norm_ffn, w_ffn_in, w_ffn_out, norm_mix, w_in, w_conv_a, b_conv_a, norm_conv_a,
           w_out_a, w_conv_b, w_out_b, norm_q, norm_k, pe_cmp, w_phi, w_out_c, w_o):
    batch, seq, d = x_prompt.shape
    ns, t_new, _ = x_sample.shape
    depth = norm_mix.shape[0]
    n_pool = cache_cmp_kv.shape[1]
    n_pages = page_table.shape[1]
    past = n_pages * PAGE_SIZE
    w_buf = state_win_kv.shape[2]
    assert seq % SEL_TILE == 0 and seq >= WINDOW + Q_BLOCK and past % L_SEL == 0

    p = _prepare_params(norm_ffn, w_ffn_in, w_ffn_out, norm_mix, w_in, w_conv_a, b_conv_a, norm_conv_a,
                        w_out_a, w_conv_b, w_out_b, norm_q, norm_k, pe_cmp, w_phi, w_out_c, w_o)
    cache_cmp = cache_cmp_kv.transpose(0, 1, 3, 4, 5, 2).reshape(depth, n_pool, 2 * KV_W, PAGE_SIZE)
    cache_sel = cache_sel_kv.transpose(0, 1, 3, 4, 5, 2).reshape(depth, n_pool, 2 * KV_W, PAGE_SIZE)
    win_state = state_win_kv.transpose(0, 1, 3, 4, 5, 2).reshape(depth, ns, 2 * KV_W, w_buf)

    def new_token_tile(rows):
        tile = rows.reshape(ns, t_new, 2 * KV_W).transpose(0, 2, 1)
        return jnp.pad(tile, ((0, 0), (0, 0), (0, LANES - t_new)))
    ovt_p = _overlap_matrix(seq // D_CMP, seq // L_SEL).T
    srow = _slope_rows()
    n_sb_s = -(-(past + t_new) // L_SEL)
    ov_s = _overlap_matrix(past // D_CMP, n_sb_s)
    expand_s = _block_expand_matrix(past + LANES)

    xp = x_prompt.reshape(batch * seq, d)
    xs = x_sample.reshape(ns * t_new, d)
    outs = [[] for _ in range(10)]
    for l in range(depth):
        xp = _ffn(xp, p["norm_ffn"], p["w_ffn_in"], p["w_ffn_out"], l, 0)
        xs = _ffn(xs, p["norm_ffn"], p["w_ffn_in"], p["w_ffn_out"], l, 0)

        a, bg, cx, kvc, kvs, kvw, qt, gtt, ksa, vst, kwa, vwt = _inproj(xp, p, l, seq)
        ca, yb = _conv_prompt(a, cx, bg, p, l, batch, seq)
        kca, vct = _compress_prompt(kvc, p, l, batch, seq)
        oc = _attn_prompt(qt, gtt, kca, vct, ovt_p, srow, ksa, vst, kwa, vwt, batch, seq)
        xp = _merge(xp, ca, yb, oc, p, l)
        outs[0].append(kvc.reshape(batch, seq, 2, N_KV, HEAD_DIM))
        outs[2].append(kvs.reshape(batch, seq, 2, N_KV, HEAD_DIM))
        outs[4].append(kvw.reshape(batch, seq, 2, N_KV, HEAD_DIM)[:, seq - min(WINDOW, seq):])
        outs[6].append(a.reshape(batch, seq, D_A)[:, seq - (CONV_A - 1):])
        outs[8].append(cx.reshape(batch, seq, D_B)[:, seq - (CONV_B - 1):])

        a, bg, cx, kvc, kvs, kvw, q, gates = _inproj(xs, p, l)
        ca, yb, new_a, new_b = _conv_sample(
            state_conv_a[l].reshape(ns, (CONV_A - 1) * D_A), a.reshape(ns, t_new * D_A),
            state_conv_b[l].reshape(ns, (CONV_B - 1) * D_B), cx.reshape(ns, t_new * D_B),
            bg.reshape(ns, t_new * D_B), p, l, t_new)
        kvc_c = _compress_sample(cache_cmp, page_table, p, l)
        qg = q.reshape(ns, t_new, N_KV, GROUP, HEAD_DIM).transpose(0, 2, 3, 1, 4)
        qg = qg.reshape(ns, N_KV, GROUP * t_new, HEAD_DIM)
        gg = gates[:, :N_BRANCH * N_HEADS].reshape(ns, t_new, N_KV, GROUP, N_BRANCH).transpose(0, 2, 3, 1, 4)
        gg = gg.reshape(ns, N_KV, GROUP * t_new, N_BRANCH)
        og, new_win = _attn_sample(qg, gg, kvc_c, ov_s, expand_s, new_token_tile(kvs), new_token_tile(kvw),
                                   win_state, cache_sel, page_table, l, t_new)
        oc = og.reshape(ns, N_KV, GROUP, t_new, HEAD_DIM).transpose(0, 3, 1, 2, 4)
        oc = oc.reshape(ns * t_new, N_HEADS * HEAD_DIM).astype(BF16)
        xs = _merge(xs, ca.reshape(ns * t_new, D_A), yb.reshape(ns * t_new, D_B), oc, p, l)
        outs[1].append(kvc.reshape(ns, t_new, 2, N_KV, HEAD_DIM))
        outs[3].append(kvs.reshape(ns, t_new, 2, N_KV, HEAD_DIM))
        outs[5].append(new_win.reshape(ns, 2, N_KV, HEAD_DIM, w_buf).transpose(0, 4, 1, 2, 3))
        outs[7].append(new_a.reshape(ns, CONV_A - 1, D_A))
        outs[9].append(new_b.reshape(ns, CONV_B - 1, D_B))

        xp = _ffn(xp, p["norm_ffn"], p["w_ffn_in"], p["w_ffn_out"], l, 1)
        xs = _ffn(xs, p["norm_ffn"], p["w_ffn_in"], p["w_ffn_out"], l, 1)

    return (xp.reshape(batch, seq, d), xs.reshape(ns, t_new, d)) + tuple(jnp.stack(o) for o in outs)
```

```python
import functools

import numpy as np
import jax
import jax.numpy as jnp
from jax import lax
from jax.experimental import pallas as pl
from jax.experimental.pallas import tpu as pltpu

F32 = jnp.float32
BF16 = jnp.bfloat16

D_A = 512
CONV_A = 31
D_B = 512
CONV_B = 3
N_HEADS = 16
N_KV = 4
GROUP = N_HEADS // N_KV
HEAD_DIM = 64
KV_W = N_KV * HEAD_DIM
L_CMP = 32
D_CMP = 16
L_SEL = 64
N_SEL = 16
N_LOCAL = 2
WINDOW = 512
Q_BLOCK = 128
N_BRANCH = 3
PAGE_SIZE = 128
EPS = 1e-6
NEG = -1e30
FORCE = 1e4
SLOPES = [float(v) for v in np.exp2(-8.0 * np.arange(1, N_HEADS + 1) / N_HEADS).astype(np.float32)]

LANES = 128
SEL_TILE = 512
N_FF_TILES = 2
SAMPLES_PER_COMPRESS_STEP = 4
VMEM_LIMIT = 52 * 1024 * 1024


def _dot(a, b):
    return jnp.dot(a, b, preferred_element_type=F32)


def _dot_nt(a, b):
    return lax.dot_general(a, b, (((1,), (1,)), ((), ())), preferred_element_type=F32)


def _dot_split(a, b):
    hi = a.astype(BF16)
    lo = (a - hi.astype(F32)).astype(BF16)
    return _dot(hi, b) + _dot(lo, b)


def _sigmoid(x):
    return 1.0 / (1.0 + jnp.exp(-x))


def _rms(x, g):
    r = lax.rsqrt(jnp.mean(x * x, axis=-1, keepdims=True) + EPS)
    return (x * r) * g


def _head_rms(z, g):
    rows, n = z.shape
    lo = lax.broadcasted_iota(jnp.int32, (rows, LANES), 1) < HEAD_DIM
    outs = []
    for c in range(n // LANES):
        zc = z[:, c * LANES:(c + 1) * LANES]
        sq = zc * zc
        s_lo = jnp.sum(jnp.where(lo, sq, 0.0), axis=1, keepdims=True)
        s_hi = jnp.sum(jnp.where(lo, 0.0, sq), axis=1, keepdims=True)
        ms = jnp.where(lo, s_lo, s_hi) * (1.0 / HEAD_DIM)
        outs.append((zc * lax.rsqrt(ms + EPS)) * g[:, c * LANES:(c + 1) * LANES])
    return outs[0] if len(outs) == 1 else jnp.concatenate(outs, axis=1)


def _topk_mask(v, k, axis):
    n = v.shape[axis]
    idx = lax.broadcasted_iota(jnp.int32, v.shape, axis).astype(F32)
    sel = jnp.zeros(v.shape, F32)
    for _ in range(k):
        m = jnp.max(v, axis=axis, keepdims=True)
        cand = jnp.where(v == m, idx, float(n))
        first = jnp.min(cand, axis=axis, keepdims=True)
        pick = idx == first
        sel = jnp.where(pick, 1.0, sel)
        v = jnp.where(pick, -3e38, v)
    return sel


def _forced_importance(imp, blk, cur, n_sb):
    local = jnp.where(blk > cur - N_LOCAL, FORCE, imp)
    imp = jnp.where(blk == 0, FORCE, jnp.where(blk <= cur, local, -1.0))
    return jnp.where(blk < n_sb, imp, -2.0)


def _cparams(sem):
    return pltpu.CompilerParams(dimension_semantics=sem, vmem_limit_bytes=VMEM_LIMIT)


def _ffn_kernel(x_ref, g_ref, wu_ref, wv_ref, wo_ref, o_ref, h_ref, acc_ref):
    f = pl.program_id(1)

    @pl.when(f == 0)
    def _():
        h_ref[...] = _rms(x_ref[...], g_ref[...]).astype(BF16)
        acc_ref[...] = jnp.zeros_like(acc_ref)

    h = h_ref[...]
    u = _dot(h, wu_ref[...])
    v = _dot(h, wv_ref[...])
    act = (u * _sigmoid(u)) * v
    acc_ref[...] += _dot(act.astype(BF16), wo_ref[...])

    @pl.when(f == pl.num_programs(1) - 1)
    def _():
        o_ref[...] = x_ref[...] + 0.5 * acc_ref[...]


def _ffn(x, norm_ffn, w_in, w_out, l, hf):
    m, d = x.shape
    ff = w_out.shape[2]
    tf = ff // N_FF_TILES
    tm = min(512, m)
    return pl.pallas_call(
        _ffn_kernel,
        grid=(m // tm, N_FF_TILES),
        in_specs=[
            pl.BlockSpec((tm, d), lambda i, f: (i, 0)),
            pl.BlockSpec((None, None, 1, d), lambda i, f: (l, hf, 0, 0)),
            pl.BlockSpec((None, None, d, tf), lambda i, f: (l, hf, 0, f)),
            pl.BlockSpec((None, None, d, tf), lambda i, f: (l, hf, 0, N_FF_TILES + f)),
            pl.BlockSpec((None, None, tf, d), lambda i, f: (l, hf, f, 0)),
        ],
        out_specs=pl.BlockSpec((tm, d), lambda i, f: (i, 0)),
        out_shape=jax.ShapeDtypeStruct((m, d), F32),
        scratch_shapes=[pltpu.VMEM((tm, d), BF16), pltpu.VMEM((tm, d), F32)],
        compiler_params=_cparams(("parallel", "arbitrary")),
        name="ffn_half",
    )(x, norm_ffn, w_in, w_in, w_out)


def _pos_columns(pos):
    lane = lax.broadcasted_iota(jnp.int32, (pos.shape[0], LANES), 1)
    hi = ((pos // L_SEL) * L_SEL).astype(F32)
    lo = (pos % L_SEL).astype(F32)
    return jnp.where(lane < HEAD_DIM + 3, hi, lo) * jnp.where(lane < HEAD_DIM, 0.0, jnp.where(lane < HEAD_DIM + 6, 1.0, 0.0))


def _augmented_keys(kn, pos_cols):
    lane = lax.broadcasted_iota(jnp.int32, pos_cols.shape, 1)
    out = []
    for g in range(kn.shape[1] // HEAD_DIM):
        v = kn[:, (g // 2) * LANES:(g // 2 + 1) * LANES]
        if g % 2:
            v = pltpu.roll(v, HEAD_DIM, 1)
        out.append(jnp.where(lane < HEAD_DIM, v, pos_cols).astype(BF16))
    return out


def _inproj_kernel(x_ref, gm_ref, wa_ref, wb_ref, wq_ref, wkv_ref, wg_ref, nq_ref, nk1_ref, nk2_ref,
                   a_ref, bg_ref, cx_ref, kvc_ref, kvs_ref, kvw_ref, *attn_refs, seq):
    tm = x_ref.shape[0]
    h = _rms(x_ref[...], gm_ref[...]).astype(BF16)
    za = _dot(h, wa_ref[...])
    a_ref[...] = za[:, :D_A] * _sigmoid(za[:, D_A:])
    zb = _dot(h, wb_ref[...])
    bg_ref[...] = zb[:, :D_B]
    cx_ref[...] = zb[:, D_B:2 * D_B] * zb[:, 2 * D_B:]
    qn = _head_rms(_dot(h, wq_ref[...]), nq_ref[...])
    zkv = _dot(h, wkv_ref[...])
    kvc_ref[...] = zkv[:, :2 * KV_W]
    ks = _head_rms(zkv[:, 2 * KV_W:3 * KV_W], nk1_ref[...])
    vs = zkv[:, 3 * KV_W:4 * KV_W]
    kw = _head_rms(zkv[:, 4 * KV_W:5 * KV_W], nk2_ref[...])
    vw = zkv[:, 5 * KV_W:]
    gates = _sigmoid(_dot(h, wg_ref[...]))
    if seq is None:
        q_ref, gt_ref = attn_refs
        kvs_ref[:, :KV_W] = ks
        kvs_ref[:, KV_W:] = vs
        kvw_ref[:, :KV_W] = kw
        kvw_ref[:, KV_W:] = vw
        q_ref[...] = qn.astype(BF16)
        gt_ref[...] = gates
        return
    kvct_ref, qt_ref, gtt_ref, ksa_ref, vst_ref, kwa_ref, vwt_ref = attn_refs
    vs_t = vs.T
    vw_t = vw.T
    kvct_ref[...] = zkv[:, :2 * KV_W].T
    kvs_ref[0:KV_W, :] = ks.T
    kvs_ref[KV_W:, :] = vs_t
    kvw_ref[0:KV_W, :] = kw.T
    kvw_ref[KV_W:, :] = vw_t
    qt_ref[...] = qn.T.astype(BF16)
    gtt_ref[...] = gates.T
    pos = (pl.program_id(0) % (seq // tm)) * tm + lax.broadcasted_iota(jnp.int32, (tm, 1), 0)
    pos_cols = _pos_columns(pos)
    lane = lax.broadcasted_iota(jnp.int32, (tm, LANES), 1)
    onehot = jnp.where(lane == pos // L_SEL, 1.0, 0.0).astype(BF16)
    for g, blk in enumerate(_augmented_keys(ks, pos_cols)):
        ksa_ref[:, 2 * g * LANES:(2 * g + 1) * LANES] = blk
        ksa_ref[:, (2 * g + 1) * LANES:(2 * g + 2) * LANES] = onehot
    for g, blk in enumerate(_augmented_keys(kw, pos_cols)):
        kwa_ref[:, g * LANES:(g + 1) * LANES] = blk
    vst_ref[0] = vs_t.astype(BF16)
    vwt_ref[...] = vw_t.astype(BF16)


def _inproj(x, p, l, seq=None):
    m, d = x.shape
    tm = min(256, m)

    def wspec(w):
        return pl.BlockSpec((None,) + w.shape[1:], lambda i: (l, 0, 0))

    def rows(n, dt):
        return pl.BlockSpec((tm, n), lambda i: (i, 0)), jax.ShapeDtypeStruct((m, n), dt)

    def cols(n, dt):
        return pl.BlockSpec((n, tm), lambda i: (0, i)), jax.ShapeDtypeStruct((n, m), dt)

    outs = [rows(D_A, F32), rows(D_B, F32), rows(D_B, F32), rows(2 * KV_W, F32)]
    if seq is None:
        outs += [rows(2 * KV_W, F32), rows(2 * KV_W, F32), rows(N_HEADS * HEAD_DIM, BF16), rows(LANES, F32)]
    else:
        assert seq % tm == 0 and seq // L_SEL <= LANES
        nt = seq // tm
        state_t = (pl.BlockSpec((None, 2 * KV_W, tm), lambda i: (i // nt, 0, i % nt)),
                   jax.ShapeDtypeStruct((m // seq, 2 * KV_W, seq), F32))
        outs += [state_t, state_t, state_t,
                 cols(N_HEADS * HEAD_DIM, BF16), cols(LANES, F32), rows(2 * N_KV * LANES, BF16),
                 (pl.BlockSpec((1, KV_W, tm), lambda i: (i, 0, 0)), jax.ShapeDtypeStruct((m // tm, KV_W, tm), BF16)),
                 rows(N_KV * LANES, BF16), cols(KV_W, BF16)]
    ws = [p["norm_mix"], p["w_a"], p["w_b"], p["w_q"], p["w_kv"], p["w_g"], p["nq"], p["nk1"], p["nk2"]]
    return pl.pallas_call(
        functools.partial(_inproj_kernel, seq=seq),
        grid=(m // tm,),
        in_specs=[pl.BlockSpec((tm, d), lambda i: (i, 0))] + [wspec(w) for w in ws],
        out_specs=[o[0] for o in outs],
        out_shape=[o[1] for o in outs],
        compiler_params=_cparams(("parallel",)),
        name="mixer_inproj",
    )(x, *ws)


CONV_HALO = 32
CONV_CHUNK = 64


def _conv_p_kernel(a_ref, ah_ref, cx_ref, ch_ref, bg_ref, wa_ref, ba_ref, ga_ref, wb_ref,
                   ca_ref, yb_ref, xa_ref, xb_ref):
    first = pl.program_id(1) == 0
    tc = a_ref.shape[0]
    xa_ref[0:CONV_HALO, :] = jnp.where(first, 0.0, ah_ref[...])
    xa_ref[CONV_HALO:, :] = a_ref[...]
    xb_ref[0:CONV_HALO, :] = jnp.where(first, 0.0, ch_ref[...])
    xb_ref[CONV_HALO:, :] = cx_ref[...]
    for c in range(tc // CONV_CHUNK):
        r0 = c * CONV_CHUNK
        acc = None
        for k in range(CONV_A):
            o = r0 + CONV_HALO - (CONV_A - 1) + k
            term = wa_ref[k:k + 1, :] * xa_ref[o:o + CONV_CHUNK, :]
            acc = term if acc is None else acc + term
        y = _rms(acc + ba_ref[...], ga_ref[...])
        ca_ref[r0:r0 + CONV_CHUNK, :] = (y * _sigmoid(y)).astype(BF16)
        acc = None
        for k in range(CONV_B):
            o = r0 + CONV_HALO - (CONV_B - 1) + k
            term = wb_ref[k:k + 1, :] * xb_ref[o:o + CONV_CHUNK, :]
            acc = term if acc is None else acc + term
        yb_ref[r0:r0 + CONV_CHUNK, :] = (bg_ref[r0:r0 + CONV_CHUNK, :] * acc).astype(BF16)


def _conv_prompt(a, cx, bg, p, l, batch, seq):
    tc = 256
    nt = seq // tc
    hb = tc // CONV_HALO

    def cur(b, i):
        return (b * nt + i, 0)

    def halo(b, i):
        return (jnp.maximum((b * nt + i) * hb - 1, 0), 0)

    def wspec(w):
        return pl.BlockSpec((None,) + w.shape[1:], lambda b, i: (l, 0, 0))

    ws = [p["w_conv_a"], p["b_conv_a"], p["norm_conv_a"], p["w_conv_b"]]
    m = batch * seq
    return pl.pallas_call(
        _conv_p_kernel,
        grid=(batch, nt),
        in_specs=[pl.BlockSpec((tc, D_A), cur), pl.BlockSpec((CONV_HALO, D_A), halo),
                  pl.BlockSpec((tc, D_B), cur), pl.BlockSpec((CONV_HALO, D_B), halo),
                  pl.BlockSpec((tc, D_B), cur)] + [wspec(w) for w in ws],
        out_specs=[pl.BlockSpec((tc, D_A), cur), pl.BlockSpec((tc, D_B), cur)],
        out_shape=[jax.ShapeDtypeStruct((m, D_A), BF16), jax.ShapeDtypeStruct((m, D_B), BF16)],
        scratch_shapes=[pltpu.VMEM((tc + CONV_HALO, D_A), F32), pltpu.VMEM((tc + CONV_HALO, D_B), F32)],
        compiler_params=_cparams(("parallel", "arbitrary")),
        name="conv_prompt",
    )(a, a, cx, cx, bg, *ws)


def _conv_s_kernel(bufa_ref, a_ref, bufb_ref, cx_ref, bg_ref, wa_ref, ba_ref, ga_ref, wb_ref,
                   ca_ref, yb_ref, na_ref, nb_ref, *, t_new):
    def col(j, w):
        return slice(j * w, (j + 1) * w)

    def up_a(j):
        return bufa_ref[:, col(j, D_A)] if j < CONV_A - 1 else a_ref[:, col(j - (CONV_A - 1), D_A)]

    def up_b(j):
        return bufb_ref[:, col(j, D_B)] if j < CONV_B - 1 else cx_ref[:, col(j - (CONV_B - 1), D_B)]

    for t in range(t_new):
        acc = None
        for k in range(CONV_A):
            term = wa_ref[k:k + 1, :] * up_a(t + k)
            acc = term if acc is None else acc + term
        y = _rms(acc + ba_ref[...], ga_ref[...])
        ca_ref[:, col(t, D_A)] = (y * _sigmoid(y)).astype(BF16)
        acc = None
        for k in range(CONV_B):
            term = wb_ref[k:k + 1, :] * up_b(t + k)
            acc = term if acc is None else acc + term
        yb_ref[:, col(t, D_B)] = (bg_ref[:, col(t, D_B)] * acc).astype(BF16)
    for j in range(CONV_A - 1):
        na_ref[:, col(j, D_A)] = up_a(j + t_new)
    for j in range(CONV_B - 1):
        nb_ref[:, col(j, D_B)] = up_b(j + t_new)


def _conv_sample(bufa, a, bufb, cx, bg, p, l, t_new):
    ns = a.shape[0]
    tn = min(64, ns)

    def rows(n):
        return pl.BlockSpec((tn, n), lambda i: (i, 0))

    def wspec(w):
        return pl.BlockSpec((None,) + w.shape[1:], lambda i: (l, 0, 0))

    ws = [p["w_conv_a"], p["b_conv_a"], p["norm_conv_a"], p["w_conv_b"]]
    ins = [bufa, a, bufb, cx, bg]
    return pl.pallas_call(
        functools.partial(_conv_s_kernel, t_new=t_new),
        grid=(ns // tn,),
        in_specs=[rows(x.shape[1]) for x in ins] + [wspec(w) for w in ws],
        out_specs=[rows(a.shape[1]), rows(cx.shape[1]), rows(bufa.shape[1]), rows(bufb.shape[1])],
        out_shape=[jax.ShapeDtypeStruct(a.shape, BF16), jax.ShapeDtypeStruct(cx.shape, BF16),
                   jax.ShapeDtypeStruct(bufa.shape, F32), jax.ShapeDtypeStruct(bufb.shape, F32)],
        compiler_params=_cparams(("parallel",)),
        name="conv_sample",
    )(*ins, *ws)


N_COL_BLOCKS = 2 * KV_W // LANES


def _compress_cols(x_ref, r_ref, pe_ref, nk_ref, lhs_ref, z_ref, n_chunks, is_k):
    z = None
    for m in range(D_CMP):
        lhs_ref[0:n_chunks, :] = x_ref[pl.ds(m, n_chunks, stride=D_CMP), :].astype(BF16)
        lhs_ref[n_chunks:, :] = pe_ref[m]
        zz = _dot(lhs_ref[...], r_ref[m])
        z = zz if z is None else z + zz
    z_ref[...] = z
    pe_bias = z_ref[n_chunks:n_chunks + 1, 0:LANES] + z_ref[n_chunks + 1:n_chunks + 2, LANES:]
    comp = z_ref[0:n_chunks, 0:LANES] + z_ref[1:n_chunks + 1, LANES:] + pe_bias
    return _head_rms(comp, nk_ref[...]) if is_k else comp


def _compress_p_kernel(x_ref, r_ref, pe_ref, nk_ref, kca_ref, vct_ref, lhs_ref, z_ref):
    n_chunks = kca_ref.shape[0]
    half = N_COL_BLOCKS // 2

    @pl.when(pl.program_id(1) < half)
    def _():
        kn = _compress_cols(x_ref, r_ref, pe_ref, nk_ref, lhs_ref, z_ref, n_chunks, True)
        c_end = lax.broadcasted_iota(jnp.int32, (n_chunks, 1), 0) * D_CMP + (L_CMP - 1)
        for g, blk in enumerate(_augmented_keys(kn, _pos_columns(c_end))):
            kca_ref[:, g * LANES:(g + 1) * LANES] = blk

    @pl.when(pl.program_id(1) >= half)
    def _():
        comp = _compress_cols(x_ref, r_ref, pe_ref, nk_ref, lhs_ref, z_ref, n_chunks, False)
        vct_ref[...] = comp.T.astype(BF16)


def _compress_prompt(kvc, p, l, batch, seq):
    nck = seq // D_CMP
    half = N_COL_BLOCKS // 2
    return pl.pallas_call(
        _compress_p_kernel,
        grid=(batch, N_COL_BLOCKS),
        in_specs=[pl.BlockSpec((seq, LANES), lambda b, c: (b, c)),
                  pl.BlockSpec((None, None, D_CMP, LANES, 2 * LANES), lambda b, c: (l, c // half, 0, 0, 0)),
                  pl.BlockSpec((None, None, D_CMP, 16, LANES), lambda b, c: (l, c // half, 0, 0, 0)),
                  pl.BlockSpec((None, 1, LANES), lambda b, c: (l, 0, 0))],
        out_specs=[pl.BlockSpec((None, nck, 2 * LANES), lambda b, c: (b, 0, jnp.minimum(c, half - 1))),
                   pl.BlockSpec((None, LANES, nck), lambda b, c: (b, jnp.maximum(c - half, 0), 0))],
        out_shape=[jax.ShapeDtypeStruct((batch, nck, N_KV * LANES), BF16),
                   jax.ShapeDtypeStruct((batch, KV_W, nck), BF16)],
        scratch_shapes=[pltpu.VMEM((nck + 16, LANES), BF16), pltpu.VMEM((nck + 16, 2 * LANES), F32)],
        compiler_params=_cparams(("arbitrary", "arbitrary")),
        name="compress_prompt",
    )(kvc, p["r_phi"], p["pe16"], p["nk0"])


def _compress_s_kernel(pt_ref, cache_ref, r_ref, pe_ref, nk_ref, o_ref, pg_ref, x_ref, lhs_ref, z_ref, sem,
                       *, layer, n_pages, n_samples):
    j = pl.program_id(0)

    def page_copy(i):
        page = pt_ref[j * n_samples * n_pages + i]
        return pltpu.make_async_copy(cache_ref.at[layer, page], pg_ref.at[i], sem.at[0])

    for i in range(n_samples * n_pages):
        page_copy(i).start()
    for i in range(n_samples * n_pages):
        page_copy(i).wait()
    for i in range(n_samples * n_pages):
        for c in range(N_COL_BLOCKS):
            x_ref[c, i * PAGE_SIZE:(i + 1) * PAGE_SIZE, :] = pg_ref[i, c * LANES:(c + 1) * LANES, :].T
    n_chunks = o_ref.shape[0]
    half = N_COL_BLOCKS // 2
    for c in range(N_COL_BLOCKS):
        o_ref[:, c * LANES:(c + 1) * LANES] = _compress_cols(
            x_ref.at[c], r_ref.at[c // half], pe_ref.at[c // half], nk_ref, lhs_ref, z_ref,
            n_chunks, c < half).astype(BF16)


def _compress_sample(cache, page_table, p, l):
    ns, n_pages = page_table.shape
    spb = min(SAMPLES_PER_COMPRESS_STEP, ns)
    rows = spb * n_pages * PAGE_SIZE
    nck = rows // D_CMP
    grid_spec = pltpu.PrefetchScalarGridSpec(
        num_scalar_prefetch=1,
        grid=(ns // spb,),
        in_specs=[pl.BlockSpec(memory_space=pl.ANY),
                  pl.BlockSpec((None, 2, D_CMP, LANES, 2 * LANES), lambda j, pt: (l, 0, 0, 0, 0)),
                  pl.BlockSpec((None, 2, D_CMP, 16, LANES), lambda j, pt: (l, 0, 0, 0, 0)),
                  pl.BlockSpec((None, 1, LANES), lambda j, pt: (l, 0, 0))],
        out_specs=pl.BlockSpec((nck, 2 * KV_W), lambda j, pt: (j, 0)),
        scratch_shapes=[pltpu.VMEM((spb * n_pages, 2 * KV_W, PAGE_SIZE), F32),
                        pltpu.VMEM((N_COL_BLOCKS, rows, LANES), F32), pltpu.VMEM((nck + 16, LANES), BF16),
                        pltpu.VMEM((nck + 16, 2 * LANES), F32), pltpu.SemaphoreType.DMA((1,))],
    )
    return pl.pallas_call(
        functools.partial(_compress_s_kernel, layer=l, n_pages=n_pages, n_samples=spb),
        grid_spec=grid_spec,
        out_shape=jax.ShapeDtypeStruct((ns * n_pages * PAGE_SIZE // D_CMP, 2 * KV_W), BF16),
        compiler_params=_cparams(("arbitrary",)),
        name="compress_sample",
    )(page_table.reshape(-1), cache, p["r_phi"], p["pe16"], p["nk0"])


N_WIN_BLOCKS = WINDOW // Q_BLOCK + 1
Q_AUG_ROWS = 2 * LANES
SLOPE_ROWS = 16


def _attn_p_kernel(qt_ref, gtt_ref, kca_ref, vct_ref, ovt_ref, srow_ref, ksa_ref, vst_ref, *rest, n_sb):
    kw_refs = rest[:N_WIN_BLOCKS]
    vw_refs = rest[N_WIN_BLOCKS:2 * N_WIN_BLOCKS]
    o_ref, qa_ref, acc_ref, m_ref, l_ref, ocw_ref, flag_ref = rest[2 * N_WIN_BLOCKS:]
    tq = qt_ref.shape[1]
    cols = GROUP * tq
    ncp = kca_ref.shape[0]
    n_tiles = ksa_ref.shape[0] // SEL_TILE
    qi = pl.program_id(1)
    t0 = qi * tq
    tok = t0 + lax.broadcasted_iota(jnp.int32, (1, tq), 1)
    tok4 = jnp.concatenate([tok] * GROUP, axis=1)

    def gate_row(g, br):
        rows = [N_BRANCH * (g * GROUP + r) + br for r in range(GROUP)]
        return jnp.concatenate([gtt_ref[c:c + 1, :] for c in rows], axis=1)

    @pl.when((pl.program_id(0) == 0) & (qi == 0))
    def _():
        for g in range(N_KV):
            qa_ref[g, HEAD_DIM:HEAD_DIM + SLOPE_ROWS, :] = srow_ref[g]
            qa_ref[g, HEAD_DIM + SLOPE_ROWS:LANES, :] = jnp.zeros((LANES - HEAD_DIM - SLOPE_ROWS, cols), BF16)

    for g in range(N_KV):
        for r in range(GROUP):
            h = g * GROUP + r
            qa_ref[g, 0:HEAD_DIM, r * tq:(r + 1) * tq] = qt_ref[h * HEAD_DIM:(h + 1) * HEAD_DIM, :]

    cpos = lax.broadcasted_iota(jnp.int32, (ncp, 1), 0) * D_CMP + (L_CMP - 1)
    cmask = cpos <= tok4
    imps = []
    for g in range(N_KV):
        s = _dot(kca_ref[:, g * LANES:(g + 1) * LANES], qa_ref[g, 0:LANES, :])
        s = jnp.where(cmask, s, NEG)
        p = jnp.where(cmask, jnp.exp(s - jnp.max(s, axis=0, keepdims=True)), 0.0)
        den = jnp.sum(p, axis=0, keepdims=True)
        pc = p / jnp.where(den > 0, den, 1.0)
        psum = pc[:, 0:tq]
        for r in range(1, GROUP):
            psum = psum + pc[:, r * tq:(r + 1) * tq]
        hi = psum.astype(BF16)
        lo = (psum - hi.astype(F32)).astype(BF16)
        imps.append(_dot(ovt_ref[...], hi) + _dot(ovt_ref[...], lo))
        ocw_ref[g] = gate_row(g, 0) * _dot(vct_ref[g * HEAD_DIM:(g + 1) * HEAD_DIM, :], pc.astype(BF16))

    blk = lax.broadcasted_iota(jnp.int32, (LANES, 1), 0)
    cur = tok4 // L_SEL
    imp = _forced_importance(jnp.concatenate(imps, axis=1), blk, cur, n_sb)
    sel = jnp.where(blk <= cur, _topk_mask(imp, min(N_SEL, n_sb), 0), 0.0)
    selb = jnp.where(sel > 0.5, 0.0, NEG).astype(BF16)
    blocks_per_tile = SEL_TILE // L_SEL
    for g in range(N_KV):
        qa_ref[g, LANES:, :] = jnp.concatenate([selb[:, g * tq:(g + 1) * tq]] * GROUP, axis=1)
        for kt in range(n_tiles):
            used = jnp.max(sel[kt * blocks_per_tile:(kt + 1) * blocks_per_tile, g * tq:(g + 1) * tq])
            flag_ref[g * n_tiles + kt] = (used > 0.5).astype(jnp.int32)

    wpos = t0 - WINDOW + lax.broadcasted_iota(jnp.int32, (N_WIN_BLOCKS * tq, 1), 0)
    wdist = tok4 - wpos
    wbias = jnp.where(wpos >= 0, jnp.where(wdist >= 0, jnp.where(wdist <= WINDOW, 0.0, NEG), NEG), NEG)
    for g in range(N_KV):
        kwin = jnp.concatenate([kr[:, g * LANES:(g + 1) * LANES] for kr in kw_refs], axis=0)
        vwin = jnp.concatenate([vr[g * HEAD_DIM:(g + 1) * HEAD_DIM, :] for vr in vw_refs], axis=1)
        s = _dot(kwin, qa_ref[g, 0:LANES, :]) + wbias
        p = jnp.exp(s - jnp.max(s, axis=0, keepdims=True))
        den = jnp.sum(p, axis=0, keepdims=True)
        ocw_ref[g] += gate_row(g, 2) * (_dot(vwin, p.astype(BF16)) / den)

    m_ref[...] = jnp.full(m_ref.shape, -1e29, F32)
    l_ref[...] = jnp.zeros(l_ref.shape, F32)
    acc_ref[...] = jnp.zeros(acc_ref.shape, F32)
    half = SEL_TILE // 2

    def tile_update(kt, g, extra):
        k0 = pl.multiple_of(kt * SEL_TILE, SEL_TILE)
        s = _dot(ksa_ref[pl.ds(k0, SEL_TILE), 2 * g * LANES:(2 * g + 2) * LANES], qa_ref[g])
        if extra is not None:
            s = s + extra
        m_prev = m_ref[g]
        m_new = jnp.maximum(m_prev, jnp.max(s, axis=0, keepdims=True))
        p = jnp.exp(s - m_new)
        alpha = jnp.exp(m_prev - m_new)
        l_ref[g] = alpha * l_ref[g] + jnp.sum(p, axis=0, keepdims=True)
        pb = p.astype(BF16)
        pv = _dot(vst_ref[2 * kt, g * HEAD_DIM:(g + 1) * HEAD_DIM, :], pb[0:half])
        pv += _dot(vst_ref[2 * kt + 1, g * HEAD_DIM:(g + 1) * HEAD_DIM, :], pb[half:])
        acc_ref[g] = acc_ref[g] * alpha + pv
        m_ref[g] = m_new

    n_full = t0 // SEL_TILE

    def sweep(kt, carry):
        for g in range(N_KV):
            @pl.when(flag_ref[g * n_tiles + kt] > 0)
            def _():
                tile_update(kt, g, None)
        return carry

    lax.fori_loop(0, n_full, sweep, 0)
    kpos = n_full * SEL_TILE + lax.broadcasted_iota(jnp.int32, (SEL_TILE, 1), 0)
    causal = jnp.where(kpos <= tok4, 0.0, NEG)
    for g in range(N_KV):
        tile_update(n_full, g, causal)

    for g in range(N_KV):
        den = l_ref[g]
        o_t = ocw_ref[g] + gate_row(g, 1) * (acc_ref[g] / jnp.where(den > 0, den, 1.0))
        for pair in range(GROUP // 2):
            two = jnp.concatenate([o_t[:, (2 * pair) * tq:(2 * pair + 1) * tq],
                                   o_t[:, (2 * pair + 1) * tq:(2 * pair + 2) * tq]], axis=0)
            c0 = (g * GROUP + 2 * pair) * HEAD_DIM
            o_ref[:, c0:c0 + 2 * HEAD_DIM] = two.T.astype(o_ref.dtype)


def _attn_prompt(qt, gtt, kca, vct, ovt, srow, ksa, vst, kwa, vwt, batch, seq):
    tq = Q_BLOCK
    nqb = seq // tq
    nck = seq // D_CMP
    n_sb = seq // L_SEL
    chunks = vst.shape[0] // batch
    assert vst.shape[2] * 2 == SEL_TILE and 2 * HEAD_DIM == LANES

    def qcol(b, i):
        return (0, b * nqb + i)

    def win_rows(j):
        return lambda b, i: (b * nqb + jnp.maximum(i - (N_WIN_BLOCKS - 1) + j, 0), 0)

    def win_cols(j):
        return lambda b, i: (0, b * nqb + jnp.maximum(i - (N_WIN_BLOCKS - 1) + j, 0))

    in_specs = [pl.BlockSpec((N_HEADS * HEAD_DIM, tq), qcol),
                pl.BlockSpec((LANES, tq), qcol),
                pl.BlockSpec((None, nck, N_KV * LANES), lambda b, i: (b, 0, 0)),
                pl.BlockSpec((None, KV_W, nck), lambda b, i: (b, 0, 0)),
                pl.BlockSpec(ovt.shape, lambda b, i: (0, 0)),
                pl.BlockSpec(srow.shape, lambda b, i: (0, 0, 0)),
                pl.BlockSpec((seq, 2 * N_KV * LANES), lambda b, i: (b, 0), pipeline_mode=pl.Buffered(1)),
                pl.BlockSpec((chunks, KV_W, vst.shape[2]), lambda b, i: (b, 0, 0), pipeline_mode=pl.Buffered(1))]
    in_specs += [pl.BlockSpec((tq, N_KV * LANES), win_rows(j)) for j in range(N_WIN_BLOCKS)]
    in_specs += [pl.BlockSpec((KV_W, tq), win_cols(j)) for j in range(N_WIN_BLOCKS)]
    cols = GROUP * tq
    return pl.pallas_call(
        functools.partial(_attn_p_kernel, n_sb=n_sb),
        grid=(batch, nqb),
        in_specs=in_specs,
        out_specs=pl.BlockSpec((tq, N_HEADS * HEAD_DIM), lambda b, i: (b * nqb + i, 0)),
        out_shape=jax.ShapeDtypeStruct((batch * seq, N_HEADS * HEAD_DIM), BF16),
        scratch_shapes=[pltpu.VMEM((N_KV, Q_AUG_ROWS, cols), BF16), pltpu.VMEM((N_KV, HEAD_DIM, cols), F32),
                        pltpu.VMEM((N_KV, 1, cols), F32), pltpu.VMEM((N_KV, 1, cols), F32),
                        pltpu.VMEM((N_KV, HEAD_DIM, cols), F32),
                        pltpu.SMEM((N_KV * (seq // SEL_TILE),), jnp.int32)],
        compiler_params=_cparams(("arbitrary", "arbitrary")),
        name="nsa_prompt",
    )(qt, gtt, kca, vct, ovt, srow, ksa, vst, *([kwa] * N_WIN_BLOCKS), *([vwt] * N_WIN_BLOCKS))


SAMPLES_PER_ATTN_STEP = 4


def _attn_s_kernel(pt_ref, q_ref, gt_ref, kvc_ref, ov_ref, ex_ref, kvs_ref, kvw_ref, win_ref, cache_ref,
                   o_ref, nwin_ref, ksel_ref, imp_ref, sem, *, layer, n_pages, t_new, n_sb):
    spb = q_ref.shape[0]
    n0 = pl.program_id(0) * spb
    past = n_pages * PAGE_SIZE
    w_buf = win_ref.shape[2]
    rows = GROUP * t_new
    gt_rows = N_KV * t_new
    ncp = kvc_ref.shape[0] // spb

    def page_copy(s, pg):
        dst = ksel_ref.at[s, :, pl.ds(pg * PAGE_SIZE, PAGE_SIZE)]
        return pltpu.make_async_copy(cache_ref.at[layer, pt_ref[(n0 + s) * n_pages + pg]], dst, sem.at[0])

    for s in range(spb):
        for pg in range(n_pages):
            page_copy(s, pg).start()

    kwin = {}
    for s in range(spb):
        kwin[s] = jnp.concatenate([win_ref[s], kvw_ref[s]], axis=1)
        nwin_ref[s] = pltpu.roll(kwin[s], kwin[s].shape[1] - t_new, 1)[:, 0:w_buf]

    ridx = lax.broadcasted_iota(jnp.int32, (rows, 1), 0)
    tpos = past + ridx % t_new

    def head_slopes(g):
        out = jnp.zeros((rows, 1), F32)
        for r in range(GROUP):
            out = jnp.where(ridx // t_new == r, SLOPES[g * GROUP + r], out)
        return out

    def attend(scores, kpos, mask, sl):
        s = scores - sl * (tpos - kpos).astype(F32)
        s = jnp.where(mask, s, NEG)
        p = jnp.where(mask, jnp.exp(s - jnp.max(s, axis=1, keepdims=True)), 0.0)
        den = jnp.sum(p, axis=1, keepdims=True)
        return p / jnp.where(den > 0, den, 1.0)

    def group_rows(x, g):
        return x[g * HEAD_DIM:(g + 1) * HEAD_DIM, :].astype(BF16), \
            x[KV_W + g * HEAD_DIM:KV_W + (g + 1) * HEAD_DIM, :].astype(BF16)

    cpos = lax.broadcasted_iota(jnp.int32, (1, ncp), 1) * D_CMP + (L_CMP - 1)
    o_cmp = {}
    for s in range(spb):
        for g in range(N_KV):
            kcg = kvc_ref[s * ncp:(s + 1) * ncp, g * HEAD_DIM:(g + 1) * HEAD_DIM]
            vcg = kvc_ref[s * ncp:(s + 1) * ncp, KV_W + g * HEAD_DIM:KV_W + (g + 1) * HEAD_DIM]
            pc = attend(_dot_nt(q_ref[s, g], kcg), cpos, cpos <= tpos, head_slopes(g))
            o_cmp[s, g] = _dot(pc.astype(BF16), vcg)
            imp_rows = _dot_split(pc, ov_ref[...])
            imp = imp_rows[0:t_new]
            for r in range(1, GROUP):
                imp = imp + imp_rows[r * t_new:(r + 1) * t_new]
            imp_ref[s * gt_rows + g * t_new:s * gt_rows + (g + 1) * t_new, :] = imp

    tsel = past + lax.broadcasted_iota(jnp.int32, (spb * gt_rows, 1), 0) % t_new
    blk = lax.broadcasted_iota(jnp.int32, (1, LANES), 1)
    cur = tsel // L_SEL
    imp = _forced_importance(imp_ref[...], blk, cur, n_sb)
    sel = _topk_mask(imp, min(N_SEL, n_sb), 1)
    selb = jnp.where(blk <= cur, jnp.where(sel > 0.5, 0.0, NEG), NEG).astype(BF16)
    key_bias = _dot(selb, ex_ref[...]).astype(BF16)

    n_w = w_buf + kvw_ref.shape[2]
    wpos = past - w_buf + lax.broadcasted_iota(jnp.int32, (1, n_w), 1)
    wdist = tpos - wpos
    wmask = (wdist >= 0) & (wdist <= WINDOW)
    o_win = {}
    for s in range(spb):
        for g in range(N_KV):
            kt, vt = group_rows(kwin[s], g)
            pw = attend(_dot(q_ref[s, g], kt), wpos, wmask, head_slopes(g))
            o_win[s, g] = _dot_nt(pw.astype(BF16), vt)

    for s in range(spb):
        for pg in range(n_pages):
            page_copy(s, pg).wait()
    n_k = ksel_ref.shape[2]
    kpos = lax.broadcasted_iota(jnp.int32, (1, n_k), 1)
    col = lax.broadcasted_iota(jnp.int32, (rows, spb * gt_rows), 1)
    for s in range(spb):
        ksel_ref[s, :, past:] = kvs_ref[s]
        for g in range(N_KV):
            spread = jnp.where(col == s * gt_rows + g * t_new + ridx % t_new, 1.0, 0.0).astype(BF16)
            bias = _dot(spread, key_bias)
            mask = (bias > 0.5 * NEG) & (kpos <= tpos)
            kt, vt = group_rows(ksel_ref.at[s], g)
            ps = attend(_dot(q_ref[s, g], kt), kpos, mask, head_slopes(g))
            o_sel = _dot_nt(ps.astype(BF16), vt)
            gates = gt_ref[s, g]
            o_ref[s, g] = gates[:, 0:1] * o_cmp[s, g] + gates[:, 1:2] * o_sel + gates[:, 2:3] * o_win[s, g]


def _attn_sample(q, gates, kvc, ov, expand, kvs_new, kvw_new, win, cache, page_table, l, t_new):
    ns, n_pages = page_table.shape
    spb = min(SAMPLES_PER_ATTN_STEP, ns)
    past = n_pages * PAGE_SIZE
    w_buf = win.shape[3]
    n_sb = -(-(past + t_new) // L_SEL)
    n_k = past + kvs_new.shape[2]
    ncp = past // D_CMP
    rows = GROUP * t_new

    def per_sample(*blk):
        return pl.BlockSpec((spb,) + blk, lambda n, pt: (n,) + (0,) * len(blk))

    grid_spec = pltpu.PrefetchScalarGridSpec(
        num_scalar_prefetch=1,
        grid=(ns // spb,),
        in_specs=[per_sample(N_KV, rows, HEAD_DIM), per_sample(N_KV, rows, N_BRANCH),
                  pl.BlockSpec((spb * ncp, 2 * KV_W), lambda n, pt: (n, 0)),
                  pl.BlockSpec(ov.shape, lambda n, pt: (0, 0)),
                  pl.BlockSpec(expand.shape, lambda n, pt: (0, 0)),
                  per_sample(*kvs_new.shape[1:]), per_sample(*kvw_new.shape[1:]),
                  pl.BlockSpec((None, spb, 2 * KV_W, w_buf), lambda n, pt: (l, n, 0, 0)),
                  pl.BlockSpec(memory_space=pl.ANY)],
        out_specs=[per_sample(N_KV, rows, HEAD_DIM), per_sample(2 * KV_W, w_buf)],
        scratch_shapes=[pltpu.VMEM((spb, 2 * KV_W, n_k), F32),
                        pltpu.VMEM((spb * N_KV * t_new, LANES), F32), pltpu.SemaphoreType.DMA((1,))],
    )
    return pl.pallas_call(
        functools.partial(_attn_s_kernel, layer=l, n_pages=n_pages, t_new=t_new, n_sb=n_sb),
        grid_spec=grid_spec,
        out_shape=[jax.ShapeDtypeStruct((ns, N_KV, rows, HEAD_DIM), F32),
                   jax.ShapeDtypeStruct((ns, 2 * KV_W, w_buf), F32)],
        compiler_params=_cparams(("arbitrary",)),
        name="nsa_sample",
    )(page_table.reshape(-1), q, gates, kvc, ov, expand, kvs_new, kvw_new, win, cache)


def _merge_kernel(x_ref, gm_ref, ca_ref, yb_ref, oc_ref, wm_ref, woa_ref, wob_ref, woc_ref, wo_ref, o_ref):
    x = x_ref[...]
    d = x.shape[1]
    h = _rms(x, gm_ref[...]).astype(BF16)
    mix = _sigmoid(_dot(h, wm_ref[:, 0:d])) * _dot(ca_ref[...], woa_ref[...])
    mix += _sigmoid(_dot(h, wm_ref[:, d:2 * d])) * _dot(yb_ref[...], wob_ref[...])
    mix += _sigmoid(_dot(h, wm_ref[:, 2 * d:])) * _dot(oc_ref[...], woc_ref[...])
    o_ref[...] = x + _dot(mix.astype(BF16), wo_ref[...])


def _merge(x, ca, yb, oc, p, l):
    m, d = x.shape
    tm = min(256, m)

    def wspec(w):
        return pl.BlockSpec((None,) + w.shape[1:], lambda i: (l, 0, 0))

    def rows(a):
        return pl.BlockSpec((tm, a.shape[1]), lambda i: (i, 0))

    ws = [p["w_m"], p["w_out_a"], p["w_out_b"], p["w_out_c"], p["w_o"]]
    return pl.pallas_call(
        _merge_kernel,
        grid=(m // tm,),
        in_specs=[rows(x), wspec(p["norm_mix"]), rows(ca), rows(yb), rows(oc)] + [wspec(w) for w in ws],
        out_specs=rows(x),
        out_shape=jax.ShapeDtypeStruct((m, d), F32),
        compiler_params=_cparams(("parallel",)),
        name="mixer_merge",
    )(x, p["norm_mix"], ca, yb, oc, *ws)


def _prepare_params(norm_ffn, w_ffn_in, w_ffn_out, norm_mix, w_in, w_conv_a, b_conv_a, norm_conv_a,
                    w_out_a, w_conv_b, w_out_b, norm_q, norm_k, pe_cmp, w_phi, w_out_c, w_o):
    depth, d = norm_mix.shape
    sizes = (2 * D_A, 3 * D_B, N_HEADS * HEAD_DIM, 6 * KV_W, N_BRANCH * N_HEADS, N_BRANCH * d)
    offs = [0] + [int(v) for v in np.cumsum(sizes)]
    seg = [w_in[:, :, offs[i]:offs[i + 1]].astype(BF16) for i in range(6)]
    w_g = jnp.pad(seg[4], ((0, 0), (0, 0), (0, LANES - sizes[4])))
    gpb = LANES // HEAD_DIM
    w = w_phi.reshape(depth, 2, L_CMP // D_CMP, D_CMP, HEAD_DIM, HEAD_DIM)
    r_phi = jnp.einsum("leimdf,gh->lemgdihf", w, jnp.eye(gpb, dtype=w.dtype))
    r_phi = r_phi.reshape(depth, 2, D_CMP, LANES, 2 * LANES).astype(BF16)
    pe = pe_cmp.reshape(depth, L_CMP // D_CMP, D_CMP, 2, HEAD_DIM).transpose(0, 3, 2, 1, 4)
    pe = jnp.tile(pe, (1, 1, 1, 1, gpb))
    pe16 = jnp.pad(pe, ((0, 0), (0, 0), (0, 0), (0, 16 - L_CMP // D_CMP), (0, 0))).astype(BF16)
    return {
        "norm_ffn": norm_ffn.reshape(depth, 2, 1, d),
        "w_ffn_in": w_ffn_in.astype(BF16), "w_ffn_out": w_ffn_out.astype(BF16),
        "norm_mix": norm_mix.reshape(depth, 1, d),
        "w_a": seg[0], "w_b": seg[1], "w_q": seg[2], "w_kv": seg[3], "w_g": w_g, "w_m": seg[5],
        "nq": jnp.tile(norm_q * HEAD_DIM ** -0.5, (1, N_HEADS)).reshape(depth, 1, N_HEADS * HEAD_DIM),
        "nk0": jnp.tile(norm_k[:, 0], (1, LANES // HEAD_DIM)).reshape(depth, 1, LANES),
        "nk1": jnp.tile(norm_k[:, 1], (1, N_KV)).reshape(depth, 1, KV_W),
        "nk2": jnp.tile(norm_k[:, 2], (1, N_KV)).reshape(depth, 1, KV_W),
        "w_conv_a": w_conv_a, "b_conv_a": b_conv_a.reshape(depth, 1, D_A),
        "norm_conv_a": norm_conv_a.reshape(depth, 1, D_A), "w_conv_b": w_conv_b,
        "w_out_a": w_out_a.astype(BF16), "w_out_b": w_out_b.astype(BF16),
        "w_out_c": w_out_c.astype(BF16), "w_o": w_o.astype(BF16),
        "r_phi": r_phi, "pe16": pe16,
    }


def _overlap_matrix(n_chunk_rows, n_sb):
    c = np.arange(n_chunk_rows)[:, None] * D_CMP
    j = np.arange(LANES)[None, :] * L_SEL
    ov = (c < j + L_SEL) & (c + L_CMP > j) & (np.arange(LANES)[None, :] < n_sb)
    ov &= (np.arange(n_chunk_rows)[:, None] < n_chunk_rows - 1)
    return jnp.asarray(ov.astype(np.float32), dtype=BF16)


def _block_expand_matrix(n_keys):
    e = np.arange(LANES)[:, None] == (np.arange(n_keys)[None, :] // L_SEL)
    return jnp.asarray(e.astype(np.float32), dtype=BF16)


def _slope_rows():
    rows = np.zeros((N_KV, SLOPE_ROWS, GROUP * Q_BLOCK), np.float32)
    for h, s in enumerate(SLOPES):
        rest = np.float32(s)
        for i in range(3):
            piece = np.float32(rest.astype(jnp.bfloat16))
            rest = np.float32(rest - piece)
            rows[h // GROUP, [i, i + 3], (h % GROUP) * Q_BLOCK:(h % GROUP + 1) * Q_BLOCK] = piece
        assert rest == 0.0
    return jnp.asarray(rows, dtype=BF16)


def kernel(x_prompt, x_sample, cache_cmp_kv, cache_sel_kv, state_win_kv, state_conv_a, state_conv_b,
           page_table, norm_ffn, w_ffn_in, w_ffn_out, norm_mix, w_in, w_conv_a, b_conv_a, norm_conv_a,
           w_out_a, w_conv_b, w_out_b, norm_q, norm_k, pe_cmp, w_phi, w_out_c, w_o):
    batch, seq, d = x_prompt.shape
    ns, t_new, _ = x_sample.shape
    depth = norm_mix.shape[0]
    n_pool = cache_cmp_kv.shape[1]
    n_pages = page_table.shape[1]
    past = n_pages * PAGE_SIZE
    w_buf = state_win_kv.shape[2]
    assert seq % SEL_TILE == 0 and seq >= WINDOW + Q_BLOCK and past % L_SEL == 0

    p = _prepare_params(norm_ffn, w_ffn_in, w_ffn_out, norm_mix, w_in, w_conv_a, b_conv_a, norm_conv_a,
                        w_out_a, w_conv_b, w_out_b, norm_q, norm_k, pe_cmp, w_phi, w_out_c, w_o)
    cache_cmp = cache_cmp_kv.transpose(0, 1, 3, 4, 5, 2).reshape(depth, n_pool, 2 * KV_W, PAGE_SIZE)
    cache_sel = cache_sel_kv.transpose(0, 1, 3, 4, 5, 2).reshape(depth, n_pool, 2 * KV_W, PAGE_SIZE)
    win_state = state_win_kv.transpose(0, 1, 3, 4, 5, 2).reshape(depth, ns, 2 * KV_W, w_buf)

    def new_token_tile(rows):
        tile = rows.reshape(ns, t_new, 2 * KV_W).transpose(0, 2, 1)
        return jnp.pad(tile, ((0, 0), (0, 0), (0, LANES - t_new)))
    ovt_p = _overlap_matrix(seq // D_CMP, seq // L_SEL).T
    srow = _slope_rows()
    n_sb_s = -(-(past + t_new) // L_SEL)
    ov_s = _overlap_matrix(past // D_CMP, n_sb_s)
    expand_s = _block_expand_matrix(past + LANES)

    xp = x_prompt.reshape(batch * seq, d)
    xs = x_sample.reshape(ns * t_new, d)
    outs = [[] for _ in range(10)]
    for l in range(depth):
        xp = _ffn(xp, p["norm_ffn"], p["w_ffn_in"], p["w_ffn_out"], l, 0)
        xs = _ffn(xs, p["norm_ffn"], p["w_ffn_in"], p["w_ffn_out"], l, 0)

        a, bg, cx, kvc, kvs_t, kvw_t, kvc_t, qt, gtt, ksa, vst, kwa, vwt = _inproj(xp, p, l, seq)
        ca, yb = _conv_prompt(a, cx, bg, p, l, batch, seq)
        kca, vct = _compress_prompt(kvc, p, l, batch, seq)
        oc = _attn_prompt(qt, gtt, kca, vct, ovt_p, srow, ksa, vst, kwa, vwt, batch, seq)
        xp = _merge(xp, ca, yb, oc, p, l)
        def token_major(state_t):
            return state_t.reshape(batch, 2, N_KV, HEAD_DIM, -1).transpose(0, 4, 1, 2, 3)

        outs[0].append(token_major(kvc_t))
        outs[2].append(token_major(kvs_t))
        outs[4].append(token_major(kvw_t[:, :, seq - min(WINDOW, seq):]))
        outs[6].append(a.reshape(batch, seq, D_A)[:, seq - (CONV_A - 1):])
        outs[8].append(cx.reshape(batch, seq, D_B)[:, seq - (CONV_B - 1):])

        a, bg, cx, kvc, kvs, kvw, q, gates = _inproj(xs, p, l)
        ca, yb, new_a, new_b = _conv_sample(
            state_conv_a[l].reshape(ns, (CONV_A - 1) * D_A), a.reshape(ns, t_new * D_A),
            state_conv_b[l].reshape(ns, (CONV_B - 1) * D_B), cx.reshape(ns, t_new * D_B),
            bg.reshape(ns, t_new * D_B), p, l, t_new)
        kvc_c = _compress_sample(cache_cmp, page_table, p, l)
        qg = q.reshape(ns, t_new, N_KV, GROUP, HEAD_DIM).transpose(0, 2, 3, 1, 4)
        qg = qg.reshape(ns, N_KV, GROUP * t_new, HEAD_DIM)
        gg = gates[:, :N_BRANCH * N_HEADS].reshape(ns, t_new, N_KV, GROUP, N_BRANCH).transpose(0, 2, 3, 1, 4)
        gg = gg.reshape(ns, N_KV, GROUP * t_new, N_BRANCH)
        og, new_win = _attn_sample(qg, gg, kvc_c, ov_s, expand_s, new_token_tile(kvs), new_token_tile(kvw),
                                   win_state, cache_sel, page_table, l, t_new)
        oc = og.reshape(ns, N_KV, GROUP, t_new, HEAD_DIM).transpose(0, 3, 1, 2, 4)
        oc = oc.reshape(ns * t_new, N_HEADS * HEAD_DIM).astype(BF16)
        xs = _merge(xs, ca.reshape(ns * t_new, D_A), yb.reshape(ns * t_new, D_B), oc, p, l)
        outs[1].append(kvc.reshape(ns, t_new, 2, N_KV, HEAD_DIM))
        outs[3].append(kvs.reshape(ns, t_new, 2, N_KV, HEAD_DIM))
        outs[5].append(new_win.reshape(ns, 2, N_KV, HEAD_DIM, w_buf).transpose(0, 4, 1, 2, 3))
        outs[7].append(new_a.reshape(ns, CONV_A - 1, D_A))
        outs[9].append(new_b.reshape(ns, CONV_B - 1, D_B))

        xp = _ffn(xp, p["norm_ffn"], p["w_ffn_in"], p["w_ffn_out"], l, 1)
        xs = _ffn(xs, p["norm_ffn"], p["w_ffn_in"], p["w_ffn_out"], l, 1)

    return (xp.reshape(batch, seq, d), xs.reshape(ns, t_new, d)) + tuple(jnp.stack(o) for o in outs)
```
